```python
import jax, jax.numpy as jnp
from jax import lax
import numpy as np


D_MODEL = 1024
BATCH = 8
SEQ = 4096
DEPTH = 2

HEAD_DIM = 64
N_HEADS_GDN = D_MODEL // 2 // HEAD_DIM
N_HEADS_RWKV = D_MODEL // 2 // HEAD_DIM
W_GDN = N_HEADS_GDN * HEAD_DIM
W_RWKV = N_HEADS_RWKV * HEAD_DIM
CONV_WIDTH = 4
GDN_CHUNK = 64
RWKV_DECAY_LORA = 64
RWKV_ICLR_LORA = 64
RWKV_GATE_LORA = 128
RWKV_GN_EPS = 64e-5
RWKV_IN = 3 * W_RWKV + RWKV_DECAY_LORA + RWKV_ICLR_LORA + RWKV_GATE_LORA
AB_IN = 4 * W_GDN + 2 * N_HEADS_GDN + RWKV_IN
AB_SPLITS = [3 * W_GDN, 4 * W_GDN, 4 * W_GDN + N_HEADS_GDN, 4 * W_GDN + 2 * N_HEADS_GDN]
RWKV_SPLITS = [W_RWKV, 2 * W_RWKV, 3 * W_RWKV, 3 * W_RWKV + RWKV_DECAY_LORA,
               3 * W_RWKV + RWKV_DECAY_LORA + RWKV_ICLR_LORA]
N_HEADS_RET = 8
RET_KEY_DIM = D_MODEL // N_HEADS_RET
RET_VAL_DIM = 2 * RET_KEY_DIM
W_RET_V = N_HEADS_RET * RET_VAL_DIM
RET_IN = 2 * D_MODEL + 2 * W_RET_V
RET_SPLITS = [D_MODEL, 2 * D_MODEL, 2 * D_MODEL + W_RET_V]
RET_CHUNK = 128
RET_GN_EPS = 1e-6
ROPE_BASE = 10000.0
D_FF = 4 * D_MODEL
RMS_EPS = 1e-6
N_EVEN = (DEPTH + 1) // 2
N_ODD = DEPTH // 2

kernel_name = "hybrid_gdn_rwkv7_retention_block"


def rmsnorm(x, w, eps=RMS_EPS):
    xf = x.astype(jnp.float32)
    return xf * lax.rsqrt(jnp.mean(xf * xf, axis=-1, keepdims=True) + eps) * w.astype(jnp.float32)


def l2norm(x, eps=1e-6):
    return x * lax.rsqrt(jnp.sum(x * x, axis=-1, keepdims=True) + eps)


def head_norm(y, eps):
    mean = jnp.mean(y, axis=-1, keepdims=True)
    var = jnp.mean(jnp.square(y - mean), axis=-1, keepdims=True)
    return (y - mean) * lax.rsqrt(var + eps)


def split_heads(x, n_heads):
    return x.reshape(*x.shape[:-1], n_heads, x.shape[-1] // n_heads)


def causal_depthwise_conv(x, w):
    K, C = w.shape
    return lax.conv_general_dilated(x, w[:, None, :].astype(x.dtype), window_strides=(1,),
                                    padding=[(K - 1, 0)], dimension_numbers=('NWC', 'WIO', 'NWC'),
                                    feature_group_count=C)


def gated_delta_rule(q, k, v, beta, g):
    Bsz, T, H, dk = q.shape
    dv = v.shape[-1]
    C = GDN_CHUNK
    N = T // C

    def chunks(a):
        a = jnp.moveaxis(a, 2, 1)
        return a.reshape(Bsz, H, N, C, *a.shape[3:])

    q, k, v, beta, g = (chunks(a) for a in (q * dk ** -0.5, k, v, beta, g))
    gcum = jnp.cumsum(g, axis=-1)
    causal = jnp.tril(jnp.ones((C, C), dtype=bool))
    strict = jnp.tril(jnp.ones((C, C), dtype=bool), -1)
    decay = jnp.exp(jnp.where(causal, gcum[..., :, None] - gcum[..., None, :], -jnp.inf))
    kb = k * beta[..., None]
    kkT = jnp.einsum('bhncd,bhnsd->bhncs', kb, k) * decay
    a_mat = jnp.where(strict, kkT, 0.0) + jnp.eye(C, dtype=kkT.dtype)
    rhs = jnp.concatenate([v * beta[..., None], kb * jnp.exp(gcum)[..., None]], axis=-1)
    sol = lax.linalg.triangular_solve(a_mat, rhs, left_side=True, lower=True, unit_diagonal=True)
    u, w = sol[..., :dv], sol[..., dv:]
    attn = jnp.einsum('bhncd,bhnsd->bhncs', q, k) * decay
    q_dec = q * jnp.exp(gcum)[..., None]
    k_dec = k * jnp.exp(gcum[..., -1:] - gcum)[..., None]
    blk_decay = jnp.exp(gcum[..., -1])

    def step(S, inp):
        u_c, w_c, attn_c, qd_c, kd_c, bd_c = inp
        v_new = u_c - jnp.einsum('bhck,bhkv->bhcv', w_c, S)
        o = jnp.einsum('bhck,bhkv->bhcv', qd_c, S) + jnp.einsum('bhcs,bhsv->bhcv', attn_c, v_new)
        S = S * bd_c[..., None, None] + jnp.einsum('bhck,bhcv->bhkv', kd_c, v_new)
        return S, o

    xs = tuple(jnp.moveaxis(a, 2, 0) for a in (u, w, attn, q_dec, k_dec, blk_decay))
    _, o = lax.scan(step, jnp.zeros((Bsz, H, dk, dv), jnp.float32), xs)
    o = jnp.moveaxis(o, 0, 2).reshape(Bsz, H, T, dv)
    return jnp.moveaxis(o, 1, 2)


def rwkv7_recurrence(r, w, k, v, kk, a):
    Bsz, T, H, d = r.shape

    def step(S, inp):
        r_t, w_t, k_t, v_t, kk_t, a_t = inp
        sa = jnp.einsum('bhvk,bhk->bhv', S, -kk_t)
        S = (S * w_t[:, :, None, :] + sa[..., None] * (kk_t * a_t)[:, :, None, :]
             + v_t[..., None] * k_t[:, :, None, :])
        return S, jnp.einsum('bhvk,bhk->bhv', S, r_t)

    xs = tuple(jnp.moveaxis(t, 1, 0) for t in (r, w, k, v, kk, a))
    _, y = lax.scan(step, jnp.zeros((Bsz, H, d, d), jnp.float32), xs)
    return jnp.moveaxis(y, 0, 1)


def hybrid_ab_mixer(u, w_in, conv_w, a_log, dt_bias, gdn_norm_w, mu, w0, w2, a0, a2, g2,
                    k_k, k_a, r_k, ln_w, ln_b, w_out):
    Bsz, T, _ = u.shape
    p = u @ w_in
    qkv, z, b_raw, alpha_raw, rp = jnp.split(p, AB_SPLITS, axis=-1)
    qkv = jax.nn.silu(causal_depthwise_conv(qkv, conv_w))
    q, k, v = (split_heads(t, N_HEADS_GDN) for t in jnp.split(qkv, 3, axis=-1))
    q, k = l2norm(q), l2norm(k)
    beta = jax.nn.sigmoid(b_raw)
    g = -jnp.exp(a_log.astype(jnp.float32)) * jax.nn.softplus(alpha_raw + dt_bias)
    o = gated_delta_rule(q, k, v, beta, g)
    o = rmsnorm(o, gdn_norm_w) * jax.nn.silu(split_heads(z, N_HEADS_GDN))
    o_a = o.reshape(Bsz, T, W_GDN)
    prev = jnp.pad(rp, ((0, 0), (1, 0), (0, 0)))[:, :-1]
    xs = rp + (prev - rp) * mu
    r, kr, vr, xw, xa, xg = jnp.split(xs, RWKV_SPLITS, axis=-1)
    w_log = -jax.nn.softplus(-(w0 + jnp.tanh(xw) @ w2)) - 0.5
    w_dec = jnp.exp(-jnp.exp(w_log))
    a = jax.nn.sigmoid(a0 + xa @ a2)
    gate = jax.nn.sigmoid(xg) @ g2
    kk = l2norm(split_heads(kr * k_k, N_HEADS_RWKV))
    kr = kr * (1.0 + (a - 1.0) * k_k * 0.0 + (a - 1.0) * k_a) if False else kr * (1.0 + (a - 1.0) * k_a)
    rh, kh, vh, ah, wh = (split_heads(t, N_HEADS_RWKV) for t in (r, kr, vr, a, w_dec))
    y = rwkv7_recurrence(rh, wh, kh, vh, kk, ah)
    y = head_norm(y, RWKV_GN_EPS).reshape(Bsz, T, W_RWKV) * ln_w + ln_b
    bonus = jnp.sum(rh * kh * r_k, axis=-1, keepdims=True) * vh
    o_b = (y + bonus.reshape(Bsz, T, W_RWKV)) * gate
    return jnp.concatenate([o_a, o_b], axis=-1) @ w_out


def rotary_every_two(x, pos):
    d = x.shape[-1]
    angle = 1.0 / (ROPE_BASE ** jnp.linspace(0.0, 1.0, d // 2, dtype=jnp.float32))
    theta = pos[:, None].astype(jnp.float32) * angle
    cos, sin = jnp.cos(theta)[:, None, :], jnp.sin(theta)[:, None, :]
    x1, x2 = x[..., 0::2], x[..., 1::2]
    return jnp.stack([x1 * cos - x2 * sin, x2 * cos + x1 * sin], axis=-1).reshape(x.shape)


def chunkwise_retention(q, k, v):
    Bsz, T, H, dk = q.shape
    dv = v.shape[-1]
    C = RET_CHUNK
    N = T // C
    log_gamma = jnp.log1p(-jnp.exp2(-5.0 - jnp.arange(H, dtype=jnp.float32)))

    def chunks(a):
        return jnp.moveaxis(a, 2, 1).reshape(Bsz, H, N, C, a.shape[-1])

    q, k, v = chunks(q), chunks(k), chunks(v)
    idx = jnp.arange(C, dtype=jnp.float32)
    rel = idx[:, None] - idx[None, :]
    intra = jnp.where(rel >= 0, jnp.exp(log_gamma[:, None, None] * jnp.maximum(rel, 0.0)), 0.0)
    scores = jnp.einsum('bhncd,bhnsd->bhncs', q, k) * intra[:, None]
    inner = jnp.einsum('bhncs,bhnsv->bhncv', scores, v)
    q_dec = q * jnp.exp(log_gamma[:, None] * (idx + 1.0))[:, None, :, None]
    k_dec = k * jnp.exp(log_gamma[:, None] * (C - 1.0 - idx))[:, None, :, None]
    blk = jnp.exp(log_gamma * C)[:, None, None]

    def step(S, inp):
        qd, kd, vc = inp
        o = jnp.einsum('bhck,bhkv->bhcv', qd, S)
        S = S * blk + jnp.einsum('bhck,bhcv->bhkv', kd, vc)
        return S, o

    xs = tuple(jnp.moveaxis(a, 2, 0) for a in (q_dec, k_dec, v))
    _, cross = lax.scan(step, jnp.zeros((Bsz, H, dk, dv), jnp.float32), xs)
    y = inner + jnp.moveaxis(cross, 0, 2)
    return jnp.moveaxis(y.reshape(Bsz, H, T, dv), 1, 2)


def retention_mixer(u, w_in, gn_w, w_out):
    Bsz, T, _ = u.shape
    q, k, v, g = jnp.split(u @ w_in, RET_SPLITS, axis=-1)
    pos = jnp.arange(T)
    q = rotary_every_two(split_heads(q, N_HEADS_RET), pos)
    k = rotary_every_two(split_heads(k, N_HEADS_RET), pos) * RET_KEY_DIM ** -0.5
    y = chunkwise_retention(q, k, split_heads(v, N_HEADS_RET))
    y = head_norm(y, RET_GN_EPS).reshape(Bsz, T, W_RET_V) * gn_w
    return (jax.nn.silu(g) * y) @ w_out


def setup_inputs(seed: int = 0) -> dict:
    key = jax.random.key(seed)
    ks = jax.random.split(key, 27)
    nrm = lambda k, shape, s: jax.random.normal(k, shape, jnp.float32) * s
    gain = lambda k, shape: 1.0 + 0.02 * jax.random.normal(k, shape, jnp.float32)
    col_scale = jnp.concatenate([jnp.ones((4 * W_GDN + N_HEADS_GDN,), jnp.float32),
                                 0.1 * jnp.ones((N_HEADS_GDN,), jnp.float32),
                                 jnp.ones((RWKV_IN,), jnp.float32)])
    dt = jnp.exp(jax.random.uniform(ks[9], (N_EVEN, N_HEADS_GDN), jnp.float32, np.log(1e-3), np.log(1e-1)))
    w0_base = -6.0 + 5.0 * jnp.linspace(0.0, 1.0, W_RWKV, dtype=jnp.float32) ** 0.85
    return {
        'x': nrm(ks[0], (BATCH, SEQ, D_MODEL), 1.0),
        'norm_mix_pre': gain(ks[1], (DEPTH, D_MODEL)),
        'norm_mix_post': gain(ks[2], (DEPTH, D_MODEL)),
        'norm_mlp_pre': gain(ks[3], (DEPTH, D_MODEL)),
        'norm_mlp_post': gain(ks[4], (DEPTH, D_MODEL)),
        'mlp_w_up': nrm(ks[5], (DEPTH, D_MODEL, D_FF), D_MODEL ** -0.5),
        'mlp_w_down': nrm(ks[6], (DEPTH, D_FF, D_MODEL), D_FF ** -0.5),
        'ab_w_in': nrm(ks[7], (N_EVEN, D_MODEL, AB_IN), D_MODEL ** -0.5) * col_scale,
        'gdn_conv_w': nrm(ks[8], (N_EVEN, CONV_WIDTH, 3 * W_GDN), CONV_WIDTH ** -0.5),
        'gdn_a_log': jnp.log(jax.random.uniform(ks[10], (N_EVEN, N_HEADS_GDN), jnp.float32, 1.0, 16.0)),
        'gdn_dt_bias': dt + jnp.log(-jnp.expm1(-dt)),
        'gdn_norm_w': gain(ks[11], (N_EVEN, HEAD_DIM)),
        'rwkv_mu': jax.random.uniform(ks[12], (N_EVEN, RWKV_IN), jnp.float32),
        'rwkv_w0': w0_base + 0.1 * jax.random.normal(ks[13], (N_EVEN, W_RWKV), jnp.float32),
        'rwkv_w2': nrm(ks[14], (N_EVEN, RWKV_DECAY_LORA, W_RWKV), 0.1 * RWKV_DECAY_LORA ** -0.5),
        'rwkv_a0': nrm(ks[15], (N_EVEN, W_RWKV), 0.1),
        'rwkv_a2': nrm(ks[16], (N_EVEN, RWKV_ICLR_LORA, W_RWKV), 0.1 * RWKV_ICLR_LORA ** -0.5),
        'rwkv_g2': nrm(ks[17], (N_EVEN, RWKV_GATE_LORA, W_RWKV), RWKV_GATE_LORA ** -0.5),
        'rwkv_k_k': 0.85 + 0.1 * jax.random.normal(ks[18], (N_EVEN, W_RWKV), jnp.float32),
        'rwkv_k_a': 1.0 + 0.1 * jax.random.normal(ks[19], (N_EVEN, W_RWKV), jnp.float32),
        'rwkv_r_k': nrm(ks[20], (N_EVEN, N_HEADS_RWKV, HEAD_DIM), 0.1),
        'rwkv_ln_w': gain(ks[21], (N_EVEN, W_RWKV)),
        'rwkv_ln_b': nrm(ks[22], (N_EVEN, W_RWKV), 0.02),
        'ab_w_out': nrm(ks[23], (N_EVEN, W_GDN + W_RWKV, D_MODEL), (W_GDN + W_RWKV) ** -0.5),
        'ret_w_in': nrm(ks[24], (N_ODD, D_MODEL, RET_IN), D_MODEL ** -0.5),
        'ret_gn_w': gain(ks[25], (N_ODD, W_RET_V)),
        'ret_w_out': nrm(ks[26], (N_ODD, W_RET_V, D_MODEL), W_RET_V ** -0.5),
    }


def reference(x, norm_mix_pre, norm_mix_post, norm_mlp_pre, norm_mlp_post, mlp_w_up, mlp_w_down,
              ab_w_in, gdn_conv_w, gdn_a_log, gdn_dt_bias, gdn_norm_w,
              rwkv_mu, rwkv_w0, rwkv_w2, rwkv_a0, rwkv_a2, rwkv_g2, rwkv_k_k, rwkv_k_a, rwkv_r_k,
              rwkv_ln_w, rwkv_ln_b, ab_w_out, ret_w_in, ret_gn_w, ret_w_out):
    h = x.astype(jnp.float32)
    for layer in range(DEPTH):
        j = layer // 2
        u = rmsnorm(h, norm_mix_pre[layer])
        if layer % 2 == 0:
            mix = hybrid_ab_mixer(u, ab_w_in[j], gdn_conv_w[j], gdn_a_log[j], gdn_dt_bias[j], gdn_norm_w[j],
                                  rwkv_mu[j], rwkv_w0[j], rwkv_w2[j], rwkv_a0[j], rwkv_a2[j], rwkv_g2[j],
                                  rwkv_k_k[j], rwkv_k_a[j], rwkv_r_k[j], rwkv_ln_w[j], rwkv_ln_b[j], ab_w_out[j])
        else:
            mix = retention_mixer(u, ret_w_in[j], ret_gn_w[j], ret_w_out[j])
        h = h + rmsnorm(mix, norm_mix_post[layer])
        u = rmsnorm(h, norm_mlp_pre[layer])
        f = jnp.square(jax.nn.relu(u @ mlp_w_up[layer])) @ mlp_w_down[layer]
        h = h + rmsnorm(f, norm_mlp_post[layer])
    return h.astype(x.dtype)
```

```python
import functools

import numpy as np
import jax
import jax.numpy as jnp
from jax import lax
from jax.experimental import pallas as pl
from jax.experimental.pallas import tpu as pltpu

F32 = jnp.float32
BF16 = jnp.bfloat16
MXU_DTYPE = BF16

RMS_EPS = 1e-6
L2_EPS = 1e-6
HEAD_DIM = 64
N_HEADS_AB = 8
W_AB = N_HEADS_AB * HEAD_DIM
CHUNK = 64
SUB = 16
CONV_WIDTH = 4
RWKV_GN_EPS = 64e-5
RWKV_LORA = 64
RWKV_GATE_LORA = 128
N_HEADS_RET = 8
RET_DK = 128
RET_DV = 256
RET_GN_EPS = 1e-6
ROPE_BASE = 10000.0
LANES = 128
SUBLANES = 8
VMEM_LIMIT = 56 * 1024 * 1024


def _mm(a, b):
    return jnp.dot(a.astype(MXU_DTYPE), b.astype(MXU_DTYPE), preferred_element_type=F32)


def _mm_nt(a, b):
    return lax.dot_general(a.astype(MXU_DTYPE), b.astype(MXU_DTYPE), (((1,), (1,)), ((), ())),
                           preferred_element_type=F32)


def _mm_tn(a, b):
    return lax.dot_general(a.astype(MXU_DTYPE), b.astype(MXU_DTYPE), (((0,), (0,)), ((), ())),
                           preferred_element_type=F32)


def _split(x, n):
    parts, r = [], x
    for i in range(n):
        p = r.astype(BF16)
        parts.append(p)
        if i + 1 < n:
            r = r - p.astype(F32)
    return parts


def _sel_r(x, e, n):
    acc = None
    for p in _split(x, n):
        d = jnp.dot(p, e, preferred_element_type=F32)
        acc = d if acc is None else acc + d
    return acc


def _sel_l(e, x, n):
    acc = None
    for p in _split(x, n):
        d = jnp.dot(e, p, preferred_element_type=F32)
        acc = d if acc is None else acc + d
    return acc


def _mm_hp(a, b):
    ah, al = _split(a, 2)
    bh, bl = _split(b, 2)
    return (jnp.dot(ah, bh, preferred_element_type=F32) + jnp.dot(al, bh, preferred_element_type=F32)
            + jnp.dot(ah, bl, preferred_element_type=F32))


def _iota2(shape, dim):
    return lax.broadcasted_iota(jnp.int32, shape, dim)


def _head_ones(width, head):
    r, c = _iota2((width, width), 0), _iota2((width, width), 1)
    return jnp.where((r // head) == (c // head), 1.0, 0.0).astype(BF16)


def _chunk_tril(tb, chunk):
    r, c = _iota2((tb, tb), 0), _iota2((tb, tb), 1)
    return jnp.where(((r // chunk) == (c // chunk)) & (c <= r), 1.0, 0.0).astype(BF16)


def _chunk_last(x, tb, chunk):
    return jnp.concatenate(
        [jnp.broadcast_to(x[(c + 1) * chunk - 1:(c + 1) * chunk, :], (chunk, x.shape[1]))
         for c in range(tb // chunk)], axis=0)


def _sigmoid(x):
    return 1.0 / (1.0 + jnp.exp(-x))


def _softplus(x):
    return jnp.maximum(x, 0.0) + jnp.log1p(jnp.exp(-jnp.abs(x)))


def _inv_unit_lower(a, eye, same_sub):
    d = jnp.where(same_sub, a, 0.0)
    l = a - d
    d2 = _mm_hp(d, d)
    d4 = _mm_hp(d2, d2)
    d8 = _mm_hp(d4, d4)
    p = eye - d
    p = p + _mm_hp(p, d2)
    p = p + _mm_hp(p, d4)
    td = p + _mm_hp(p, d8)
    n = _mm_hp(td, l)
    n2 = _mm_hp(n, n)
    r = eye - n
    r = r + _mm_hp(r, n2)
    return _mm_hp(r, td)


def _rms(x, g):
    return x * lax.rsqrt(jnp.mean(x * x, axis=-1, keepdims=True) + RMS_EPS) * g


def _norm_proj_kernel(x_ref, g_ref, *refs, n_out):
    u = _rms(x_ref[...], g_ref[...]).astype(MXU_DTYPE)
    for w_ref, o_ref in zip(refs[:n_out], refs[n_out:]):
        o_ref[...] = jnp.dot(u, w_ref[...], preferred_element_type=F32).astype(o_ref.dtype)


def _norm_proj(x, gain, weights, out_dtypes, tm):
    m, d = x.shape
    n_out = len(weights)
    resident = lambda shape: pl.BlockSpec(shape, lambda i: (0, 0), pipeline_mode=pl.Buffered(1))
    return pl.pallas_call(
        functools.partial(_norm_proj_kernel, n_out=n_out),
        grid=(m // tm,),
        in_specs=[pl.BlockSpec((tm, d), lambda i: (i, 0)), resident((1, d))]
                 + [resident(w.shape) for w in weights],
        out_specs=[pl.BlockSpec((tm, w.shape[1]), lambda i: (i, 0)) for w in weights],
        out_shape=[jax.ShapeDtypeStruct((m, w.shape[1]), dt) for w, dt in zip(weights, out_dtypes)],
        compiler_params=pltpu.CompilerParams(dimension_semantics=("parallel",),
                                             vmem_limit_bytes=VMEM_LIMIT),
        name="norm_proj",
    )(x, gain.reshape(1, d), *weights)


def _out_proj_kernel(*refs, n_in):
    a_refs, w_refs = refs[:n_in], refs[n_in:2 * n_in]
    h_ref, g_ref, o_ref = refs[2 * n_in:]
    acc = None
    for a_ref, w_ref in zip(a_refs, w_refs):
        d = jnp.dot(a_ref[...], w_ref[...], preferred_element_type=F32)
        acc = d if acc is None else acc + d
    o_ref[...] = h_ref[...] + _rms(acc, g_ref[...])


def _out_proj(acts, weights, h, gain, tm):
    m, d = h.shape
    n_in = len(acts)
    resident = lambda shape: pl.BlockSpec(shape, lambda i: (0, 0), pipeline_mode=pl.Buffered(1))
    return pl.pallas_call(
        functools.partial(_out_proj_kernel, n_in=n_in),
        grid=(m // tm,),
        in_specs=[pl.BlockSpec((tm, a.shape[1]), lambda i: (i, 0)) for a in acts]
                 + [resident(w.shape) for w in weights]
                 + [pl.BlockSpec((tm, d), lambda i: (i, 0)), resident((1, d))],
        out_specs=pl.BlockSpec((tm, d), lambda i: (i, 0)),
        out_shape=jax.ShapeDtypeStruct((m, d), F32),
        compiler_params=pltpu.CompilerParams(dimension_semantics=("parallel",),
                                             vmem_limit_bytes=VMEM_LIMIT),
        name="out_proj",
    )(*acts, *weights, h, gain.reshape(1, d))


def _mlp_kernel(h_ref, g1_ref, wu_ref, wd_ref, g2_ref, o_ref, *, ff_blk):
    x = h_ref[...]
    u = _rms(x, g1_ref[...]).astype(MXU_DTYPE)
    acc = None
    for j in range(wu_ref.shape[1] // ff_blk):
        a = jnp.dot(u, wu_ref[:, j * ff_blk:(j + 1) * ff_blk], preferred_element_type=F32)
        a = jnp.square(jnp.maximum(a, 0.0)).astype(MXU_DTYPE)
        d = jnp.dot(a, wd_ref[j * ff_blk:(j + 1) * ff_blk, :], preferred_element_type=F32)
        acc = d if acc is None else acc + d
    o_ref[...] = x + _rms(acc, g2_ref[...])


def _mlp(h, g1, w_up, w_down, g2, tm, ff_blk):
    m, d = h.shape
    resident = lambda shape: pl.BlockSpec(shape, lambda i: (0, 0), pipeline_mode=pl.Buffered(1))
    return pl.pallas_call(
        functools.partial(_mlp_kernel, ff_blk=ff_blk),
        grid=(m // tm,),
        in_specs=[pl.BlockSpec((tm, d), lambda i: (i, 0)), resident((1, d)),
                  resident(w_up.shape), resident(w_down.shape), resident((1, d))],
        out_specs=pl.BlockSpec((tm, d), lambda i: (i, 0)),
        out_shape=jax.ShapeDtypeStruct((m, d), F32),
        compiler_params=pltpu.CompilerParams(dimension_semantics=("parallel",),
                                             vmem_limit_bytes=VMEM_LIMIT),
        name="mlp",
    )(h, g1.reshape(1, d), w_up, w_down, g2.reshape(1, d))


def _gdn_kernel(qkv_ref, z_ref, gt_ref, cw_ref, gp_ref, nw_ref, o_ref,
                xp_ref, s_ref, k_s, kb_s, q_s, qd_s, kd_s, kbe_s, vb_s, gce_s, bd_s, gct_s, o_s,
                *, tb):
    nh, hd, c = N_HEADS_AB, HEAD_DIM, CHUNK
    nc = tb // c

    @pl.when(pl.program_id(1) == 0)
    def _():
        s_ref[...] = jnp.zeros_like(s_ref)
        xp_ref[0:SUBLANES, :] = jnp.zeros((SUBLANES, xp_ref.shape[1]), F32)

    x = qkv_ref[...]
    xp_ref[SUBLANES:SUBLANES + tb, :] = x
    cw = cw_ref[...]
    y = x * cw[CONV_WIDTH - 1:CONV_WIDTH, :]
    for j in range(CONV_WIDTH - 1):
        y = y + xp_ref[pl.ds(SUBLANES - (CONV_WIDTH - 1) + j, tb), :] * cw[j:j + 1, :]
    xp_ref[0:SUBLANES, :] = x[tb - SUBLANES:tb, :]
    y = y * _sigmoid(y)
    q, k, v = y[:, 0:W_AB], y[:, W_AB:2 * W_AB], y[:, 2 * W_AB:3 * W_AB]

    ones_h = _head_ones(W_AB, hd)
    qn = q * lax.rsqrt(_sel_r(q * q, ones_h, 2) + L2_EPS) * (hd ** -0.5)
    kn = k * lax.rsqrt(_sel_r(k * k, ones_h, 2) + L2_EPS)

    gt = gt_ref[...]
    gp = gp_ref[...]
    beta = _sigmoid(gt)
    g = -jnp.exp(gp[0:1, :]) * _softplus(gt + gp[1:2, :])
    gcum = _sel_l(_chunk_tril(tb, c), g, 3)
    r_, c_ = _iota2((LANES, W_AB), 0), _iota2((LANES, W_AB), 1)
    pick_beta = jnp.where(r_ == c_ // hd, 1.0, 0.0).astype(BF16)
    pick_g = jnp.where(r_ == nh + c_ // hd, 1.0, 0.0).astype(BF16)
    beta_e = _sel_r(beta, pick_beta, 2)
    gc_e = _sel_r(gcum, pick_g, 3)
    glast_e = _chunk_last(gc_e, tb, c)
    gct = gcum.T

    eg = jnp.exp(gc_e)
    kb = kn * beta_e
    per_head = ((k_s, kn), (kb_s, kb), (q_s, qn), (qd_s, qn * eg), (kd_s, kn * jnp.exp(glast_e - gc_e)),
                (kbe_s, kb * eg), (vb_s, v * beta_e), (gce_s, gc_e), (bd_s, jnp.exp(glast_e)))
    for dst, val in per_head:
        for h in range(nh):
            dst[h] = val[:, h * hd:(h + 1) * hd]
    for ci in range(nc):
        gct_s[ci] = gct[:, ci * c:(ci + 1) * c]

    ri, cj = _iota2((c, c), 0), _iota2((c, c), 1)
    causal, strict = ri >= cj, ri > cj
    eye = jnp.where(ri == cj, 1.0, 0.0).astype(F32)
    same_sub = (ri // SUB) == (cj // SUB)

    def chunk_body(ci, carry):
        rows = pl.ds(pl.multiple_of(ci * c, c), c)
        for h in range(nh):
            kk_, kb_, q_ = k_s[h, rows, :], kb_s[h, rows, :], q_s[h, rows, :]
            diff = gce_s[h, rows, :] - gct_s[ci, nh + h:nh + h + 1, :]
            dec = jnp.where(causal, jnp.exp(jnp.where(causal, diff, 0.0)), 0.0)
            a = jnp.where(strict, _mm_nt(kb_, kk_) * dec, 0.0)
            attn = _mm_nt(q_, kk_) * dec
            t = _inv_unit_lower(a, eye, same_sub)
            w = _mm(t, kbe_s[h, rows, :])
            u = _mm(t, vb_s[h, rows, :])
            kd_ = kd_s[h, rows, :]
            s = s_ref[h]
            o = _mm(qd_s[h, rows, :] - _mm(attn, w), s) + _mm(attn, u)
            s_ref[h] = bd_s[h, rows, :] * s - _mm(_mm_tn(kd_, w), s) + _mm_tn(kd_, u)
            o_s[rows, h * hd:(h + 1) * hd] = o
        return carry

    lax.fori_loop(0, nc, chunk_body, 0)

    o = o_s[...]
    o = o * lax.rsqrt(_sel_r(o * o, ones_h, 2) * (1.0 / hd) + RMS_EPS) * nw_ref[...]
    z = z_ref[...]
    o_ref[...] = (o * (z * _sigmoid(z))).astype(o_ref.dtype)


def _gdn(qkv, z, gates, conv_w, gate_params, norm_w, batch, tb):
    m = qkv.shape[0]
    nt = m // batch // tb
    nh, hd, c = N_HEADS_AB, HEAD_DIM, CHUNK
    row_blk = lambda n: pl.BlockSpec((tb, n), lambda b, t: (b * nt + t, 0))
    const = lambda shape: pl.BlockSpec(shape, lambda b, t: (0, 0))
    head_major = pltpu.VMEM((nh, tb, hd), F32)
    return pl.pallas_call(
        functools.partial(_gdn_kernel, tb=tb),
        grid=(batch, nt),
        in_specs=[row_blk(3 * W_AB), row_blk(W_AB), row_blk(LANES),
                  const(conv_w.shape), const(gate_params.shape), const(norm_w.shape)],
        out_specs=row_blk(W_AB),
        out_shape=jax.ShapeDtypeStruct((m, W_AB), BF16),
        scratch_shapes=[pltpu.VMEM((SUBLANES + tb, 3 * W_AB), F32), pltpu.VMEM((nh, hd, hd), F32)]
                       + [head_major] * 9
                       + [pltpu.VMEM((tb // c, LANES, c), F32), pltpu.VMEM((tb, W_AB), F32)],
        compiler_params=pltpu.CompilerParams(dimension_semantics=("parallel", "arbitrary"),
                                             vmem_limit_bytes=VMEM_LIMIT),
        name="gdn",
    )(qkv, z, gates, conv_w, gate_params, norm_w)


def _rwkv_kernel(rp_ref, mu_ref, w0_ref, w2_ref, a0_ref, a2_ref, g2_ref, kk_ref, ka_ref, rk_ref,
                 lnw_ref, lnb_ref, o_ref,
                 xp_ref, s_ref, rh_s, kkh_s, bh_s, kh_s, bt_s, kt_s, v_s, wc_s, y_s, *, tb):
    nh, hd, c = N_HEADS_AB, HEAD_DIM, CHUNK
    nc = tb // c

    @pl.when(pl.program_id(1) == 0)
    def _():
        s_ref[...] = jnp.zeros_like(s_ref)
        xp_ref[0:SUBLANES, :] = jnp.zeros((SUBLANES, xp_ref.shape[1]), F32)

    rp = rp_ref[...]
    xp_ref[SUBLANES:SUBLANES + tb, :] = rp
    prev = xp_ref[pl.ds(SUBLANES - 1, tb), :]
    xp_ref[0:SUBLANES, :] = rp[tb - SUBLANES:tb, :]
    xs = rp + (prev - rp) * mu_ref[...]
    r, kr, vr = xs[:, 0:W_AB], xs[:, W_AB:2 * W_AB], xs[:, 2 * W_AB:3 * W_AB]
    xwa = xs[:, 3 * W_AB:3 * W_AB + 2 * RWKV_LORA]
    xg = xs[:, 3 * W_AB + 2 * RWKV_LORA:]

    w_log = -_softplus(-(w0_ref[...] + _mm(jnp.tanh(xwa), w2_ref[...]))) - 0.5
    lw = -jnp.exp(w_log)
    a = _sigmoid(a0_ref[...] + _mm(xwa, a2_ref[...]))
    gate = _mm(_sigmoid(xg), g2_ref[...])

    ones_h = _head_ones(W_AB, hd)
    kk = kr * kk_ref[...]
    kk = kk * lax.rsqrt(_sel_r(kk * kk, ones_h, 2) + L2_EPS)
    k = kr * (1.0 + (a - 1.0) * ka_ref[...])
    b = kk * a

    cum = _sel_l(_chunk_tril(tb, c), lw, 3)
    last = _chunk_last(cum, tb, c)
    e_neg = jnp.exp(-cum)
    e_rem = jnp.exp(last - cum)
    per_head = ((rh_s, r * jnp.exp(cum)), (kkh_s, kk * jnp.exp(cum - lw)), (bh_s, b * e_neg),
                (kh_s, k * e_neg), (bt_s, b * e_rem), (kt_s, k * e_rem), (v_s, vr), (wc_s, jnp.exp(last)))
    for dst, val in per_head:
        for h in range(nh):
            dst[h] = val[:, h * hd:(h + 1) * hd]

    ri, cj = _iota2((c, c), 0), _iota2((c, c), 1)
    causal, strict = ri >= cj, ri > cj
    eye = jnp.where(ri == cj, 1.0, 0.0).astype(F32)
    same_sub = (ri // SUB) == (cj // SUB)

    def chunk_body(ci, carry):
        rows = pl.ds(pl.multiple_of(ci * c, c), c)
        for h in range(nh):
            rh, kkh, bh, kh = rh_s[h, rows, :], kkh_s[h, rows, :], bh_s[h, rows, :], kh_s[h, rows, :]
            bt, kt, vv = bt_s[h, rows, :], kt_s[h, rows, :], v_s[h, rows, :]
            a_bb = jnp.where(strict, _mm_nt(kkh, bh), 0.0)
            a_bk = jnp.where(strict, _mm_nt(kkh, kh), 0.0)
            a_rb = jnp.where(causal, _mm_nt(rh, bh), 0.0)
            a_rk = jnp.where(causal, _mm_nt(rh, kh), 0.0)
            t = _inv_unit_lower(a_bb, eye, same_sub)
            wk = _mm(t, kkh)
            u0 = -_mm(t, _mm(a_bk, vv))
            s = s_ref[h]
            y = _mm_nt(rh - _mm(a_rb, wk), s) + _mm(a_rb, u0) + _mm(a_rk, vv)
            s_ref[h] = (s * wc_s[h, rows, :][0:1, :] - _mm_nt(s, _mm_tn(bt, wk))
                        + _mm_tn(u0, bt) + _mm_tn(vv, kt))
            y_s[rows, h * hd:(h + 1) * hd] = y
        return carry

    lax.fori_loop(0, nc, chunk_body, 0)

    y = y_s[...]
    inv_hd = 1.0 / hd
    yc = y - _sel_r(y, ones_h, 2) * inv_hd
    yn = yc * lax.rsqrt(_sel_r(yc * yc, ones_h, 2) * inv_hd + RWKV_GN_EPS) * lnw_ref[...] + lnb_ref[...]
    bonus = _sel_r(r * k * rk_ref[...], ones_h, 2) * vr
    o_ref[...] = ((yn + bonus) * gate).astype(o_ref.dtype)


def _rwkv(rp, params, batch, tb):
    m, n_in = rp.shape
    nt = m // batch // tb
    nh, hd = N_HEADS_AB, HEAD_DIM
    row_blk = lambda n: pl.BlockSpec((tb, n), lambda b, t: (b * nt + t, 0))
    const = lambda shape: pl.BlockSpec(shape, lambda b, t: (0, 0))
    head_major = pltpu.VMEM((nh, tb, hd), F32)
    return pl.pallas_call(
        functools.partial(_rwkv_kernel, tb=tb),
        grid=(batch, nt),
        in_specs=[row_blk(n_in)] + [const(p.shape) for p in params],
        out_specs=row_blk(W_AB),
        out_shape=jax.ShapeDtypeStruct((m, W_AB), BF16),
        scratch_shapes=[pltpu.VMEM((SUBLANES + tb, n_in), F32), pltpu.VMEM((nh, hd, hd), F32)]
                       + [head_major] * 8 + [pltpu.VMEM((tb, W_AB), F32)],
        compiler_params=pltpu.CompilerParams(dimension_semantics=("parallel", "arbitrary"),
                                             vmem_limit_bytes=VMEM_LIMIT),
        name="rwkv7",
    )(rp, *params)


def _rope_table_kernel(angle_ref, cos_ref, sin_ref):
    shape = cos_ref.shape
    pos = _iota2(shape, 0).astype(F32)
    theta = pos * angle_ref[...]
    cos_ref[...] = jnp.cos(theta)
    s = jnp.sin(theta)
    sin_ref[...] = jnp.where(_iota2(shape, 1) < RET_DK // 2, -s, s)


def _rope_tables(seq):
    angle = 1.0 / (ROPE_BASE ** jnp.linspace(0.0, 1.0, RET_DK // 2, dtype=F32))
    angle = jnp.concatenate([angle, angle]).reshape(1, RET_DK)
    return pl.pallas_call(
        _rope_table_kernel,
        out_shape=[jax.ShapeDtypeStruct((seq, RET_DK), F32)] * 2,
        name="rope_tables",
    )(angle)


def _ret_log_gamma():
    return [float(np.log1p(-np.exp2(np.float32(-5.0 - h)))) for h in range(N_HEADS_RET)]


def _ret_kernel(q_ref, k_ref, v_ref, g_ref, cos_ref, sin_ref, gnw_ref, o_ref, s_ref, *, tb):
    dk, dv = RET_DK, RET_DV

    @pl.when(pl.program_id(1) == 0)
    def _():
        s_ref[...] = jnp.zeros_like(s_ref)

    cos, sin = cos_ref[...], sin_ref[...]
    ri, cj = _iota2((tb, tb), 0), _iota2((tb, tb), 1)
    rel = (ri - cj).astype(F32)
    idx = _iota2((tb, 1), 0).astype(F32)
    for h, lg in enumerate(_ret_log_gamma()):
        q = q_ref[:, h * dk:(h + 1) * dk]
        k = k_ref[:, h * dk:(h + 1) * dk]
        q = q * cos + pltpu.roll(q, dk // 2, 1) * sin
        k = (k * cos + pltpu.roll(k, dk // 2, 1) * sin) * (dk ** -0.5)
        v = v_ref[:, h * dv:(h + 1) * dv]
        intra = jnp.where(rel >= 0.0, jnp.exp(lg * jnp.maximum(rel, 0.0)), 0.0)
        scores = _mm_nt(q, k) * intra
        s = s_ref[h]
        y = _mm(scores, v) + _mm(q * jnp.exp(lg * (idx + 1.0)), s)
        s_ref[h] = s * float(np.exp(lg * tb)) + _mm_tn(k * jnp.exp(lg * (tb - 1.0 - idx)), v)
        yc = y - jnp.mean(y, axis=-1, keepdims=True)
        yn = yc * lax.rsqrt(jnp.mean(yc * yc, axis=-1, keepdims=True) + RET_GN_EPS)
        g = g_ref[:, h * dv:(h + 1) * dv]
        o_ref[:, h * dv:(h + 1) * dv] = (g * _sigmoid(g) * yn * gnw_ref[:, h * dv:(h + 1) * dv]
                                         ).astype(o_ref.dtype)


def _retention(q, k, v, g, cos, sin, gn_w, batch, tb):
    m = q.shape[0]
    nt = m // batch // tb
    row_blk = lambda n: pl.BlockSpec((tb, n), lambda b, t: (b * nt + t, 0))
    pos_blk = pl.BlockSpec((tb, RET_DK), lambda b, t: (t, 0))
    wv = N_HEADS_RET * RET_DV
    return pl.pallas_call(
        functools.partial(_ret_kernel, tb=tb),
        grid=(batch, nt),
        in_specs=[row_blk(q.shape[1]), row_blk(k.shape[1]), row_blk(wv), row_blk(wv), pos_blk, pos_blk,
                  pl.BlockSpec((1, wv), lambda b, t: (0, 0))],
        out_specs=row_blk(wv),
        out_shape=jax.ShapeDtypeStruct((m, wv), BF16),
        scratch_shapes=[pltpu.VMEM((N_HEADS_RET, RET_DK, RET_DV), F32)],
        compiler_params=pltpu.CompilerParams(dimension_semantics=("parallel", "arbitrary"),
                                             vmem_limit_bytes=VMEM_LIMIT),
        name="retention",
    )(q, k, v, g, cos, sin, gn_w)


def _tiles(m, seq):
    tm = 512 if m % 512 == 0 else 256
    tb_ab = 256 if seq % 256 == 0 else CHUNK
    tb_ret = 128
    return tm, tb_ab, tb_ret


def _row(v):
    return v.reshape(1, -1).astype(F32)


def _even_layer_mix(h, batch, seq, pre_gain, post_gain, w_in, conv_w, a_log, dt_bias, gdn_norm_w,
                    mu, w0, w2, a0, a2, g2, k_k, k_a, r_k, ln_w, ln_b, w_out):
    tm, tb, _ = _tiles(h.shape[0], seq)
    nh = N_HEADS_AB
    w_in = w_in.astype(MXU_DTYPE)
    gate_cols = jnp.pad(w_in[:, 4 * W_AB:4 * W_AB + 2 * nh], ((0, 0), (0, LANES - 2 * nh)))
    weights = [w_in[:, :3 * W_AB], w_in[:, 3 * W_AB:4 * W_AB], gate_cols, w_in[:, 4 * W_AB + 2 * nh:]]
    qkv, z, gates, rp = _norm_proj(h, pre_gain, weights, [F32] * 4, tm)

    pad_g = lambda v: jnp.pad(v.astype(F32), (nh, LANES - 2 * nh))
    gate_params = jnp.stack([pad_g(a_log), pad_g(dt_bias)])
    o_a = _gdn(qkv, z, gates, conv_w.astype(F32), gate_params,
               _row(jnp.tile(gdn_norm_w, nh)), batch, tb)

    zeros = jnp.zeros((RWKV_LORA, W_AB), F32)
    params = [_row(mu), _row(w0), jnp.concatenate([w2, zeros]).astype(MXU_DTYPE), _row(a0),
              jnp.concatenate([zeros, a2]).astype(MXU_DTYPE), g2.astype(MXU_DTYPE),
              _row(k_k), _row(k_a), _row(r_k), _row(ln_w), _row(ln_b)]
    o_b = _rwkv(rp, params, batch, tb)

    w_out = w_out.astype(MXU_DTYPE)
    return _out_proj([o_a, o_b], [w_out[:W_AB], w_out[W_AB:]], h, post_gain, tm)


def _odd_layer_mix(h, batch, seq, pre_gain, post_gain, w_in, gn_w, w_out):
    tm, _, tb = _tiles(h.shape[0], seq)
    d = h.shape[1]
    wv = N_HEADS_RET * RET_DV
    perm = np.concatenate([np.concatenate([np.arange(0, RET_DK, 2), np.arange(1, RET_DK, 2)]) + hh * RET_DK
                           for hh in range(N_HEADS_RET)])
    w_in = w_in.astype(MXU_DTYPE)
    weights = [w_in[:, :d][:, perm], w_in[:, d:2 * d][:, perm], w_in[:, 2 * d:2 * d + wv], w_in[:, 2 * d + wv:]]
    q, k, v, g = _norm_proj(h, pre_gain, weights, [F32] * 4, tm // 2)
    cos, sin = _rope_tables(seq)
    y = _retention(q, k, v, g, cos, sin, _row(gn_w), batch, tb)
    return _out_proj([y], [w_out.astype(MXU_DTYPE)], h, post_gain, tm)


def kernel(x, norm_mix_pre, norm_mix_post, norm_mlp_pre, norm_mlp_post, mlp_w_up, mlp_w_down, ab_w_in, gdn_conv_w, gdn_a_log, gdn_dt_bias, gdn_norm_w, rwkv_mu, rwkv_w0, rwkv_w2, rwkv_a0, rwkv_a2, rwkv_g2, rwkv_k_k, rwkv_k_a, rwkv_r_k, rwkv_ln_w, rwkv_ln_b, ab_w_out, ret_w_in, ret_gn_w, ret_w_out):
    batch, seq, d = x.shape
    h = x.astype(F32).reshape(batch * seq, d)
    tm = _tiles(batch * seq, seq)[0]
    for layer in range(norm_mix_pre.shape[0]):
        j = layer // 2
        if layer % 2 == 0:
            h = _even_layer_mix(h, batch, seq, norm_mix_pre[layer], norm_mix_post[layer], ab_w_in[j],
                                gdn_conv_w[j], gdn_a_log[j], gdn_dt_bias[j], gdn_norm_w[j], rwkv_mu[j],
                                rwkv_w0[j], rwkv_w2[j], rwkv_a0[j], rwkv_a2[j], rwkv_g2[j], rwkv_k_k[j],
                                rwkv_k_a[j], rwkv_r_k[j], rwkv_ln_w[j], rwkv_ln_b[j], ab_w_out[j])
        else:
            h = _odd_layer_mix(h, batch, seq, norm_mix_pre[layer], norm_mix_post[layer], ret_w_in[j],
                               ret_gn_w[j], ret_w_out[j])
        h = _mlp(h, norm_mlp_pre[layer], mlp_w_up[layer].astype(MXU_DTYPE),
                 mlp_w_down[layer].astype(MXU_DTYPE), norm_mlp_post[layer], tm, 1024)
    return h.reshape(batch, seq, d).astype(x.dtype)
```

```python
import functools

import numpy as np
import jax
import jax.numpy as jnp
from jax import lax
from jax.experimental import pallas as pl
from jax.experimental.pallas import tpu as pltpu

F32 = jnp.float32
BF16 = jnp.bfloat16
MXU_DTYPE = BF16

RMS_EPS = 1e-6
L2_EPS = 1e-6
HEAD_DIM = 64
N_HEADS_AB = 8
W_AB = N_HEADS_AB * HEAD_DIM
CHUNK = 64
SUB = 16
CONV_WIDTH = 4
RWKV_GN_EPS = 64e-5
RWKV_LORA = 64
RWKV_GATE_LORA = 128
N_HEADS_RET = 8
RET_DK = 128
RET_DV = 256
RET_GN_EPS = 1e-6
ROPE_BASE = 10000.0
LANES = 128
SUBLANES = 8
VMEM_LIMIT = 56 * 1024 * 1024


def _mm(a, b):
    return jnp.dot(a.astype(MXU_DTYPE), b.astype(MXU_DTYPE), preferred_element_type=F32)


def _mm_nt(a, b):
    return lax.dot_general(a.astype(MXU_DTYPE), b.astype(MXU_DTYPE), (((1,), (1,)), ((), ())),
                           preferred_element_type=F32)


def _mm_tn(a, b):
    return lax.dot_general(a.astype(MXU_DTYPE), b.astype(MXU_DTYPE), (((0,), (0,)), ((), ())),
                           preferred_element_type=F32)


def _split(x, n):
    parts, r = [], x
    for i in range(n):
        p = r.astype(BF16)
        parts.append(p)
        if i + 1 < n:
            r = r - p.astype(F32)
    return parts


def _sel_r(x, e, n):
    acc = None
    for p in _split(x, n):
        d = jnp.dot(p, e, preferred_element_type=F32)
        acc = d if acc is None else acc + d
    return acc


def _sel_l(e, x, n):
    acc = None
    for p in _split(x, n):
        d = jnp.dot(e, p, preferred_element_type=F32)
        acc = d if acc is None else acc + d
    return acc


def _iota2(shape, dim):
    return lax.broadcasted_iota(jnp.int32, shape, dim)


def _head_ones(width, head):
    r, c = _iota2((width, width), 0), _iota2((width, width), 1)
    return jnp.where((r // head) == (c // head), 1.0, 0.0).astype(BF16)


def _chunk_tril(tb, chunk):
    r, c = _iota2((tb, tb), 0), _iota2((tb, tb), 1)
    return jnp.where(((r // chunk) == (c // chunk)) & (c <= r), 1.0, 0.0).astype(BF16)


def _chunk_last(x, tb, chunk):
    return jnp.concatenate(
        [jnp.broadcast_to(x[(c + 1) * chunk - 1:(c + 1) * chunk, :], (chunk, x.shape[1]))
         for c in range(tb // chunk)], axis=0)


def _sigmoid(x):
    return 1.0 / (1.0 + jnp.exp(-x))


def _softplus(x):
    return jnp.maximum(x, 0.0) + jnp.log1p(jnp.exp(-jnp.abs(x)))


GROUP = 4
GROUP_W = GROUP * HEAD_DIM
N_GROUPS = N_HEADS_AB // GROUP


def _packed_masks():
    c = CHUNK
    r, l = _iota2((c, GROUP_W), 0), _iota2((c, GROUP_W), 1) % c
    rr, cc = _iota2((GROUP_W, GROUP_W), 0), _iota2((GROUP_W, GROUP_W), 1)
    return dict(causal=r >= l, strict=r > l, eye=jnp.where(r == l, 1.0, 0.0).astype(F32),
                same_sub=(r // SUB) == (l // SUB), blk=(rr // c) == (cc // c))


def _bd(x, blk):
    xb = x.astype(MXU_DTYPE)
    return jnp.where(blk, jnp.concatenate([xb] * GROUP, axis=0), jnp.zeros((), MXU_DTYPE))


def _pmm(a, ybd):
    return jnp.dot(a.astype(MXU_DTYPE), ybd, preferred_element_type=F32)


def _pmm_nt(a, ybd):
    return lax.dot_general(a.astype(MXU_DTYPE), ybd, (((1,), (1,)), ((), ())), preferred_element_type=F32)


def _pmm_hp(x, y, blk):
    xh, xl = _split(x, 2)
    yh, yl = _split(y, 2)
    ybh = _bd(yh, blk)
    hi = jnp.dot(jnp.concatenate([xh, xl], axis=0), ybh, preferred_element_type=F32)
    return hi[:x.shape[0]] + hi[x.shape[0]:] + jnp.dot(xh, _bd(yl, blk), preferred_element_type=F32)


def _inv_unit_lower(a, mk):
    hp = lambda x, y: _pmm_hp(x, y, mk["blk"])
    eye = mk["eye"]
    d = jnp.where(mk["same_sub"], a, 0.0)
    l = a - d
    d2 = hp(d, d)
    d4 = hp(d2, d2)
    d8 = hp(d4, d4)
    p = eye - d
    p = p + hp(p, d2)
    p = p + hp(p, d4)
    td = p + hp(p, d8)
    n = hp(td, l)
    n2 = hp(n, n)
    r = eye - n
    r = r + hp(r, n2)
    return hp(r, td)


def _scan_scratch(tb):
    nc = tb // CHUNK
    per = lambda rows, dt: pltpu.VMEM((nc, N_GROUPS, rows, GROUP_W), dt)
    return [pltpu.VMEM((N_GROUPS, GROUP_W, GROUP_W), F32), per(CHUNK, MXU_DTYPE), per(CHUNK, F32),
            per(GROUP_W, MXU_DTYPE), per(GROUP_W, F32), pltpu.VMEM((tb, W_AB), F32)]


def _rms(x, g):
    return x * lax.rsqrt(jnp.mean(x * x, axis=-1, keepdims=True) + RMS_EPS) * g


def _norm_proj_kernel(x_ref, g_ref, *refs, n_out):
    u = _rms(x_ref[...], g_ref[...]).astype(MXU_DTYPE)
    for w_ref, o_ref in zip(refs[:n_out], refs[n_out:]):
        o_ref[...] = jnp.dot(u, w_ref[...], preferred_element_type=F32).astype(o_ref.dtype)


def _norm_proj(x, gain, weights, out_dtypes, tm):
    m, d = x.shape
    n_out = len(weights)
    resident = lambda shape: pl.BlockSpec(shape, lambda i: (0, 0), pipeline_mode=pl.Buffered(1))
    return pl.pallas_call(
        functools.partial(_norm_proj_kernel, n_out=n_out),
        grid=(m // tm,),
        in_specs=[pl.BlockSpec((tm, d), lambda i: (i, 0)), resident((1, d))]
                 + [resident(w.shape) for w in weights],
        out_specs=[pl.BlockSpec((tm, w.shape[1]), lambda i: (i, 0)) for w in weights],
        out_shape=[jax.ShapeDtypeStruct((m, w.shape[1]), dt) for w, dt in zip(weights, out_dtypes)],
        compiler_params=pltpu.CompilerParams(dimension_semantics=("parallel",),
                                             vmem_limit_bytes=VMEM_LIMIT),
        name="norm_proj",
    )(x, gain.reshape(1, d), *weights)


def _out_proj_kernel(*refs, n_in):
    a_refs, w_refs = refs[:n_in], refs[n_in:2 * n_in]
    h_ref, g_ref, o_ref = refs[2 * n_in:]
    acc = None
    for a_ref, w_ref in zip(a_refs, w_refs):
        d = jnp.dot(a_ref[...], w_ref[...], preferred_element_type=F32)
        acc = d if acc is None else acc + d
    o_ref[...] = h_ref[...] + _rms(acc, g_ref[...])


def _out_proj(acts, weights, h, gain, tm):
    m, d = h.shape
    n_in = len(acts)
    resident = lambda shape: pl.BlockSpec(shape, lambda i: (0, 0), pipeline_mode=pl.Buffered(1))
    return pl.pallas_call(
        functools.partial(_out_proj_kernel, n_in=n_in),
        grid=(m // tm,),
        in_specs=[pl.BlockSpec((tm, a.shape[1]), lambda i: (i, 0)) for a in acts]
                 + [resident(w.shape) for w in weights]
                 + [pl.BlockSpec((tm, d), lambda i: (i, 0)), resident((1, d))],
        out_specs=pl.BlockSpec((tm, d), lambda i: (i, 0)),
        out_shape=jax.ShapeDtypeStruct((m, d), F32),
        compiler_params=pltpu.CompilerParams(dimension_semantics=("parallel",),
                                             vmem_limit_bytes=VMEM_LIMIT),
        name="out_proj",
    )(*acts, *weights, h, gain.reshape(1, d))


def _mlp_kernel(h_ref, g1_ref, wu_ref, wd_ref, g2_ref, o_ref, *, ff_blk):
    x = h_ref[...]
    u = _rms(x, g1_ref[...]).astype(MXU_DTYPE)
    acc = None
    for j in range(wu_ref.shape[1] // ff_blk):
        a = jnp.dot(u, wu_ref[:, j * ff_blk:(j + 1) * ff_blk], preferred_element_type=F32)
        a = jnp.square(jnp.maximum(a, 0.0)).astype(MXU_DTYPE)
        d = jnp.dot(a, wd_ref[j * ff_blk:(j + 1) * ff_blk, :], preferred_element_type=F32)
        acc = d if acc is None else acc + d
    o_ref[...] = x + _rms(acc, g2_ref[...])


def _mlp(h, g1, w_up, w_down, g2, tm, ff_blk):
    m, d = h.shape
    resident = lambda shape: pl.BlockSpec(shape, lambda i: (0, 0), pipeline_mode=pl.Buffered(1))
    return pl.pallas_call(
        functools.partial(_mlp_kernel, ff_blk=ff_blk),
        grid=(m // tm,),
        in_specs=[pl.BlockSpec((tm, d), lambda i: (i, 0)), resident((1, d)),
                  resident(w_up.shape), resident(w_down.shape), resident((1, d))],
        out_specs=pl.BlockSpec((tm, d), lambda i: (i, 0)),
        out_shape=jax.ShapeDtypeStruct((m, d), F32),
        compiler_params=pltpu.CompilerParams(dimension_semantics=("parallel",),
                                             vmem_limit_bytes=VMEM_LIMIT),
        name="mlp",
    )(h, g1.reshape(1, d), w_up, w_down, g2.reshape(1, d))


def _gdn_kernel(qkv_ref, z_ref, gt_ref, cw_ref, gp_ref, nw_ref, o_ref,
                xp_ref, s_ref, qeff_s, o0_s, mt_s, n0_s, o_s, *, tb):
    nh, hd, c = N_HEADS_AB, HEAD_DIM, CHUNK
    nc = tb // c

    @pl.when(pl.program_id(1) == 0)
    def _():
        s_ref[...] = jnp.zeros_like(s_ref)
        xp_ref[0:SUBLANES, :] = jnp.zeros((SUBLANES, xp_ref.shape[1]), F32)

    x = qkv_ref[...]
    xp_ref[SUBLANES:SUBLANES + tb, :] = x
    cw = cw_ref[...]
    y = x * cw[CONV_WIDTH - 1:CONV_WIDTH, :]
    for j in range(CONV_WIDTH - 1):
        y = y + xp_ref[pl.ds(SUBLANES - (CONV_WIDTH - 1) + j, tb), :] * cw[j:j + 1, :]
    xp_ref[0:SUBLANES, :] = x[tb - SUBLANES:tb, :]
    y = y * _sigmoid(y)
    q, k, v = y[:, 0:W_AB], y[:, W_AB:2 * W_AB], y[:, 2 * W_AB:3 * W_AB]

    ones_h = _head_ones(W_AB, hd)
    qn = q * lax.rsqrt(_sel_r(q * q, ones_h, 2) + L2_EPS) * (hd ** -0.5)
    kn = k * lax.rsqrt(_sel_r(k * k, ones_h, 2) + L2_EPS)

    gt = gt_ref[...]
    gp = gp_ref[...]
    beta = _sigmoid(gt)
    g = -jnp.exp(gp[0:1, :]) * _softplus(gt + gp[1:2, :])
    gcum = _sel_l(_chunk_tril(tb, c), g, 3)
    r_, c_ = _iota2((LANES, W_AB), 0), _iota2((LANES, W_AB), 1)
    pick_beta = jnp.where(r_ == c_ // hd, 1.0, 0.0).astype(BF16)
    pick_g = jnp.where(r_ == nh + c_ // hd, 1.0, 0.0).astype(BF16)
    beta_e = _sel_r(beta, pick_beta, 2)
    gc_e = _sel_r(gcum, pick_g, 3)
    glast_e = _chunk_last(gc_e, tb, c)
    gct = gcum.T

    eg = jnp.exp(gc_e)
    kb = kn * beta_e
    kbe, vb, qd = kb * eg, v * beta_e, qn * eg
    kd = kn * jnp.exp(glast_e - gc_e)
    blk_decay = jnp.exp(glast_e)

    mk = _packed_masks()
    blk = mk["blk"]

    for ci in range(nc):
        rs = slice(ci * c, (ci + 1) * c)
        g_row = jnp.concatenate([gct[nh + h:nh + h + 1, rs] for h in range(nh)], axis=1)
        for gi in range(N_GROUPS):
            ls = slice(gi * GROUP_W, (gi + 1) * GROUP_W)
            diff = gc_e[rs, ls] - g_row[:, ls]
            dec = jnp.where(mk["causal"], jnp.exp(jnp.where(mk["causal"], diff, 0.0)), 0.0)
            both = _pmm_nt(jnp.concatenate([kb[rs, ls], qn[rs, ls]], axis=0), _bd(kn[rs, ls], blk))
            a = jnp.where(mk["strict"], both[:c] * dec, 0.0)
            attn = both[c:] * dec
            t = _inv_unit_lower(a, mk)
            w = _pmm(t, _bd(kbe[rs, ls], blk))
            u = _pmm(t, _bd(vb[rs, ls], blk))
            qeff_s[ci, gi] = (qd[rs, ls] - _pmm(attn, _bd(w, blk))).astype(qeff_s.dtype)
            o0_s[ci, gi] = _pmm(attn, _bd(u, blk))
            mt_s[ci, gi] = jnp.where(blk, _mm_tn(kd[rs, ls], w), 0.0).astype(mt_s.dtype)
            n0_s[ci, gi] = jnp.where(blk, _mm_tn(kd[rs, ls], u), 0.0)

    for ci in range(nc):
        rs = slice(ci * c, (ci + 1) * c)
        for gi in range(N_GROUPS):
            ls = slice(gi * GROUP_W, (gi + 1) * GROUP_W)
            s = s_ref[gi]
            sb = s.astype(MXU_DTYPE)
            o_s[rs, ls] = jnp.dot(qeff_s[ci, gi], sb, preferred_element_type=F32) + o0_s[ci, gi]
            s_ref[gi] = (s * blk_decay[ci * c:ci * c + 1, ls]
                         - jnp.dot(mt_s[ci, gi], sb, preferred_element_type=F32) + n0_s[ci, gi])

    o = o_s[...]
    o = o * lax.rsqrt(_sel_r(o * o, ones_h, 2) * (1.0 / hd) + RMS_EPS) * nw_ref[...]
    z = z_ref[...]
    o_ref[...] = (o * (z * _sigmoid(z))).astype(o_ref.dtype)


def _gdn(qkv, z, gates, conv_w, gate_params, norm_w, batch, tb):
    m = qkv.shape[0]
    nt = m // batch // tb
    row_blk = lambda n: pl.BlockSpec((tb, n), lambda b, t: (b * nt + t, 0))
    const = lambda shape: pl.BlockSpec(shape, lambda b, t: (0, 0))
    return pl.pallas_call(
        functools.partial(_gdn_kernel, tb=tb),
        grid=(batch, nt),
        in_specs=[row_blk(3 * W_AB), row_blk(W_AB), row_blk(LANES),
                  const(conv_w.shape), const(gate_params.shape), const(norm_w.shape)],
        out_specs=row_blk(W_AB),
        out_shape=jax.ShapeDtypeStruct((m, W_AB), BF16),
        scratch_shapes=[pltpu.VMEM((SUBLANES + tb, 3 * W_AB), F32)] + _scan_scratch(tb),
        compiler_params=pltpu.CompilerParams(dimension_semantics=("parallel", "arbitrary"),
                                             vmem_limit_bytes=VMEM_LIMIT),
        name="gdn",
    )(qkv, z, gates, conv_w, gate_params, norm_w)


def _rwkv_kernel(rp_ref, mu_ref, w0_ref, w2_ref, a0_ref, a2_ref, g2_ref, kk_ref, ka_ref, rk_ref,
                 lnw_ref, lnb_ref, o_ref,
                 xp_ref, s_ref, qeff_s, y0_s, mt_s, n0_s, y_s, *, tb):
    nh, hd, c = N_HEADS_AB, HEAD_DIM, CHUNK
    nc = tb // c

    @pl.when(pl.program_id(1) == 0)
    def _():
        s_ref[...] = jnp.zeros_like(s_ref)
        xp_ref[0:SUBLANES, :] = jnp.zeros((SUBLANES, xp_ref.shape[1]), F32)

    rp = rp_ref[...]
    xp_ref[SUBLANES:SUBLANES + tb, :] = rp
    prev = xp_ref[pl.ds(SUBLANES - 1, tb), :]
    xp_ref[0:SUBLANES, :] = rp[tb - SUBLANES:tb, :]
    xs = rp + (prev - rp) * mu_ref[...]
    r, kr, vr = xs[:, 0:W_AB], xs[:, W_AB:2 * W_AB], xs[:, 2 * W_AB:3 * W_AB]
    xwa = xs[:, 3 * W_AB:3 * W_AB + 2 * RWKV_LORA]
    xg = xs[:, 3 * W_AB + 2 * RWKV_LORA:]

    w_log = -_softplus(-(w0_ref[...] + _mm(jnp.tanh(xwa), w2_ref[...]))) - 0.5
    lw = -jnp.exp(w_log)
    a = _sigmoid(a0_ref[...] + _mm(xwa, a2_ref[...]))
    gate = _mm(_sigmoid(xg), g2_ref[...])

    ones_h = _head_ones(W_AB, hd)
    kk = kr * kk_ref[...]
    kk = kk * lax.rsqrt(_sel_r(kk * kk, ones_h, 2) + L2_EPS)
    k = kr * (1.0 + (a - 1.0) * ka_ref[...])
    b = kk * a

    cum = _sel_l(_chunk_tril(tb, c), lw, 3)
    last = _chunk_last(cum, tb, c)
    e_neg = jnp.exp(-cum)
    e_rem = jnp.exp(last - cum)
    rh_all, kkh_all = r * jnp.exp(cum), kk * jnp.exp(cum - lw)
    bh_all, kh_all, bt_all, kt_all = b * e_neg, k * e_neg, b * e_rem, k * e_rem
    chunk_decay = jnp.exp(last)

    mk = _packed_masks()
    blk, causal, strict = mk["blk"], mk["causal"], mk["strict"]

    for ci in range(nc):
        rs = slice(ci * c, (ci + 1) * c)
        for gi in range(N_GROUPS):
            ls = slice(gi * GROUP_W, (gi + 1) * GROUP_W)
            rh, kkh, vv, bt = rh_all[rs, ls], kkh_all[rs, ls], vr[rs, ls], bt_all[rs, ls]
            lhs = jnp.concatenate([kkh, rh], axis=0)
            ab = _pmm_nt(lhs, _bd(bh_all[rs, ls], blk))
            ak = _pmm_nt(lhs, _bd(kh_all[rs, ls], blk))
            a_bb, a_rb = jnp.where(strict, ab[:c], 0.0), jnp.where(causal, ab[c:], 0.0)
            a_k = jnp.concatenate([jnp.where(strict, ak[:c], 0.0), jnp.where(causal, ak[c:], 0.0)], axis=0)
            t = _inv_unit_lower(a_bb, mk)
            z0 = _pmm(a_k, _bd(vv, blk))
            wk = _pmm(t, _bd(kkh, blk))
            u0 = -_pmm(t, _bd(z0[:c], blk))
            qeff_s[ci, gi] = (rh - _pmm(a_rb, _bd(wk, blk))).astype(qeff_s.dtype)
            y0_s[ci, gi] = _pmm(a_rb, _bd(u0, blk)) + z0[c:]
            mt_s[ci, gi] = jnp.where(blk, _mm_tn(wk, bt), 0.0).astype(mt_s.dtype)
            n0_s[ci, gi] = jnp.where(blk, _mm_tn(jnp.concatenate([u0, vv], axis=0),
                                                 jnp.concatenate([bt, kt_all[rs, ls]], axis=0)), 0.0)

    for ci in range(nc):
        rs = slice(ci * c, (ci + 1) * c)
        for gi in range(N_GROUPS):
            ls = slice(gi * GROUP_W, (gi + 1) * GROUP_W)
            s = s_ref[gi]
            sb = s.astype(MXU_DTYPE)
            y_s[rs, ls] = lax.dot_general(qeff_s[ci, gi], sb, (((1,), (1,)), ((), ())),
                                          preferred_element_type=F32) + y0_s[ci, gi]
            s_ref[gi] = (s * chunk_decay[ci * c:ci * c + 1, ls]
                         - jnp.dot(sb, mt_s[ci, gi], preferred_element_type=F32) + n0_s[ci, gi])

    y = y_s[...]
    inv_hd = 1.0 / hd
    yc = y - _sel_r(y, ones_h, 2) * inv_hd
    yn = yc * lax.rsqrt(_sel_r(yc * yc, ones_h, 2) * inv_hd + RWKV_GN_EPS) * lnw_ref[...] + lnb_ref[...]
    bonus = _sel_r(r * k * rk_ref[...], ones_h, 2) * vr
    o_ref[...] = ((yn + bonus) * gate).astype(o_ref.dtype)


def _rwkv(rp, params, batch, tb):
    m, n_in = rp.shape
    nt = m // batch // tb
    row_blk = lambda n: pl.BlockSpec((tb, n), lambda b, t: (b * nt + t, 0))
    const = lambda shape: pl.BlockSpec(shape, lambda b, t: (0, 0))
    return pl.pallas_call(
        functools.partial(_rwkv_kernel, tb=tb),
        grid=(batch, nt),
        in_specs=[row_blk(n_in)] + [const(p.shape) for p in params],
        out_specs=row_blk(W_AB),
        out_shape=jax.ShapeDtypeStruct((m, W_AB), BF16),
        scratch_shapes=[pltpu.VMEM((SUBLANES + tb, n_in), F32)] + _scan_scratch(tb),
        compiler_params=pltpu.CompilerParams(dimension_semantics=("parallel", "arbitrary"),
                                             vmem_limit_bytes=VMEM_LIMIT),
        name="rwkv7",
    )(rp, *params)


def _rope_table_kernel(angle_ref, cos_ref, sin_ref):
    shape = cos_ref.shape
    pos = _iota2(shape, 0).astype(F32)
    theta = pos * angle_ref[...]
    cos_ref[...] = jnp.cos(theta)
    s = jnp.sin(theta)
    sin_ref[...] = jnp.where(_iota2(shape, 1) < RET_DK // 2, -s, s)


def _rope_tables(seq):
    angle = 1.0 / (ROPE_BASE ** jnp.linspace(0.0, 1.0, RET_DK // 2, dtype=F32))
    angle = jnp.concatenate([angle, angle]).reshape(1, RET_DK)
    return pl.pallas_call(
        _rope_table_kernel,
        out_shape=[jax.ShapeDtypeStruct((seq, RET_DK), F32)] * 2,
        name="rope_tables",
    )(angle)


def _ret_log_gamma():
    return [float(np.log1p(-np.exp2(np.float32(-5.0 - h)))) for h in range(N_HEADS_RET)]


def _ret_kernel(q_ref, k_ref, v_ref, g_ref, cos_ref, sin_ref, gnw_ref, o_ref, s_ref, *, tb):
    dk, dv = RET_DK, RET_DV

    @pl.when(pl.program_id(1) == 0)
    def _():
        s_ref[...] = jnp.zeros_like(s_ref)

    cos, sin = cos_ref[...], sin_ref[...]
    ri, cj = _iota2((tb, tb), 0), _iota2((tb, tb), 1)
    rel = (ri - cj).astype(F32)
    idx = _iota2((tb, 1), 0).astype(F32)
    for h, lg in enumerate(_ret_log_gamma()):
        q = q_ref[:, h * dk:(h + 1) * dk]
        k = k_ref[:, h * dk:(h + 1) * dk]
        q = q * cos + pltpu.roll(q, dk // 2, 1) * sin
        k = (k * cos + pltpu.roll(k, dk // 2, 1) * sin) * (dk ** -0.5)
        v = v_ref[:, h * dv:(h + 1) * dv]
        intra = jnp.where(rel >= 0.0, jnp.exp(lg * jnp.maximum(rel, 0.0)), 0.0)
        scores = _mm_nt(q, k) * intra
        s = s_ref[h]
        y = _mm(scores, v) + _mm(q * jnp.exp(lg * (idx + 1.0)), s)
        s_ref[h] = s * float(np.exp(lg * tb)) + _mm_tn(k * jnp.exp(lg * (tb - 1.0 - idx)), v)
        yc = y - jnp.mean(y, axis=-1, keepdims=True)
        yn = yc * lax.rsqrt(jnp.mean(yc * yc, axis=-1, keepdims=True) + RET_GN_EPS)
        g = g_ref[:, h * dv:(h + 1) * dv]
        o_ref[:, h * dv:(h + 1) * dv] = (g * _sigmoid(g) * yn * gnw_ref[:, h * dv:(h + 1) * dv]
                                         ).astype(o_ref.dtype)


def _retention(q, k, v, g, cos, sin, gn_w, batch, tb):
    m = q.shape[0]
    nt = m // batch // tb
    row_blk = lambda n: pl.BlockSpec((tb, n), lambda b, t: (b * nt + t, 0))
    pos_blk = pl.BlockSpec((tb, RET_DK), lambda b, t: (t, 0))
    wv = N_HEADS_RET * RET_DV
    return pl.pallas_call(
        functools.partial(_ret_kernel, tb=tb),
        grid=(batch, nt),
        in_specs=[row_blk(q.shape[1]), row_blk(k.shape[1]), row_blk(wv), row_blk(wv), pos_blk, pos_blk,
                  pl.BlockSpec((1, wv), lambda b, t: (0, 0))],
        out_specs=row_blk(wv),
        out_shape=jax.ShapeDtypeStruct((m, wv), BF16),
        scratch_shapes=[pltpu.VMEM((N_HEADS_RET, RET_DK, RET_DV), F32)],
        compiler_params=pltpu.CompilerParams(dimension_semantics=("parallel", "arbitrary"),
                                             vmem_limit_bytes=VMEM_LIMIT),
        name="retention",
    )(q, k, v, g, cos, sin, gn_w)


def _tiles(m, seq):
    tm = 512 if m % 512 == 0 else 256
    tb_ab = 256 if seq % 256 == 0 else CHUNK
    tb_ret = 128
    return tm, tb_ab, tb_ret


def _row(v):
    return v.reshape(1, -1).astype(F32)


def _even_layer_mix(h, batch, seq, pre_gain, post_gain, w_in, conv_w, a_log, dt_bias, gdn_norm_w,
                    mu, w0, w2, a0, a2, g2, k_k, k_a, r_k, ln_w, ln_b, w_out):
    tm, tb, _ = _tiles(h.shape[0], seq)
    nh = N_HEADS_AB
    w_in = w_in.astype(MXU_DTYPE)
    gate_cols = jnp.pad(w_in[:, 4 * W_AB:4 * W_AB + 2 * nh], ((0, 0), (0, LANES - 2 * nh)))
    weights = [w_in[:, :3 * W_AB], w_in[:, 3 * W_AB:4 * W_AB], gate_cols, w_in[:, 4 * W_AB + 2 * nh:]]
    qkv, z, gates, rp = _norm_proj(h, pre_gain, weights, [F32] * 4, tm)

    pad_g = lambda v: jnp.pad(v.astype(F32), (nh, LANES - 2 * nh))
    gate_params = jnp.stack([pad_g(a_log), pad_g(dt_bias)])
    o_a = _gdn(qkv, z, gates, conv_w.astype(F32), gate_params,
               _row(jnp.tile(gdn_norm_w, nh)), batch, tb)

    zeros = jnp.zeros((RWKV_LORA, W_AB), F32)
    params = [_row(mu), _row(w0), jnp.concatenate([w2, zeros]).astype(MXU_DTYPE), _row(a0),
              jnp.concatenate([zeros, a2]).astype(MXU_DTYPE), g2.astype(MXU_DTYPE),
              _row(k_k), _row(k_a), _row(r_k), _row(ln_w), _row(ln_b)]
    o_b = _rwkv(rp, params, batch, tb)

    w_out = w_out.astype(MXU_DTYPE)
    return _out_proj([o_a, o_b], [w_out[:W_AB], w_out[W_AB:]], h, post_gain, tm)


def _odd_layer_mix(h, batch, seq, pre_gain, post_gain, w_in, gn_w, w_out):
    tm, _, tb = _tiles(h.shape[0], seq)
    d = h.shape[1]
    wv = N_HEADS_RET * RET_DV
    perm = np.concatenate([np.concatenate([np.arange(0, RET_DK, 2), np.arange(1, RET_DK, 2)]) + hh * RET_DK
                           for hh in range(N_HEADS_RET)])
    w_in = w_in.astype(MXU_DTYPE)
    weights = [w_in[:, :d][:, perm], w_in[:, d:2 * d][:, perm], w_in[:, 2 * d:2 * d + wv], w_in[:, 2 * d + wv:]]
    q, k, v, g = _norm_proj(h, pre_gain, weights, [F32] * 4, tm // 2)
    cos, sin = _rope_tables(seq)
    y = _retention(q, k, v, g, cos, sin, _row(gn_w), batch, tb)
    return _out_proj([y], [w_out.astype(MXU_DTYPE)], h, post_gain, tm)


def kernel(x, norm_mix_pre, norm_mix_post, norm_mlp_pre, norm_mlp_post, mlp_w_up, mlp_w_down, ab_w_in, gdn_conv_w, gdn_a_log, gdn_dt_bias, gdn_norm_w, rwkv_mu, rwkv_w0, rwkv_w2, rwkv_a0, rwkv_a2, rwkv_g2, rwkv_k_k, rwkv_k_a, rwkv_r_k, rwkv_ln_w, rwkv_ln_b, ab_w_out, ret_w_in, ret_gn_w, ret_w_out):
    batch, seq, d = x.shape
    h = x.astype(F32).reshape(batch * seq, d)
    tm = _tiles(batch * seq, seq)[0]
    for layer in range(norm_mix_pre.shape[0]):
        j = layer // 2
        if layer % 2 == 0:
            h = _even_layer_mix(h, batch, seq, norm_mix_pre[layer], norm_mix_post[layer], ab_w_in[j],
                                gdn_conv_w[j], gdn_a_log[j], gdn_dt_bias[j], gdn_norm_w[j], rwkv_mu[j],
                                rwkv_w0[j], rwkv_w2[j], rwkv_a0[j], rwkv_a2[j], rwkv_g2[j], rwkv_k_k[j],
                                rwkv_k_a[j], rwkv_r_k[j], rwkv_ln_w[j], rwkv_ln_b[j], ab_w_out[j])
        else:
            h = _odd_layer_mix(h, batch, seq, norm_mix_pre[layer], norm_mix_post[layer], ret_w_in[j],
                               ret_gn_w[j], ret_w_out[j])
        h = _mlp(h, norm_mlp_pre[layer], mlp_w_up[layer].astype(MXU_DTYPE),
                 mlp_w_down[layer].astype(MXU_DTYPE), norm_mlp_post[layer], tm, 1024)
    return h.reshape(batch, seq, d).astype(x.dtype)
```

```python
import functools

import numpy as np
import jax
import jax.numpy as jnp
from jax import lax
from jax.experimental import pallas as pl
from jax.experimental.pallas import tpu as pltpu

F32 = jnp.float32
BF16 = jnp.bfloat16
MXU_DTYPE = BF16

RMS_EPS = 1e-6
L2_EPS = 1e-6
HEAD_DIM = 64
N_HEADS_AB = 8
W_AB = N_HEADS_AB * HEAD_DIM
CHUNK = 64
SUB = 16
CONV_WIDTH = 4
RWKV_GN_EPS = 64e-5
RWKV_LORA = 64
RWKV_GATE_LORA = 128
N_HEADS_RET = 8
RET_DK = 128
RET_DV = 256
RET_GN_EPS = 1e-6
ROPE_BASE = 10000.0
LANES = 128
SUBLANES = 8
VMEM_LIMIT = 56 * 1024 * 1024


def _mm(a, b):
    return jnp.dot(a.astype(MXU_DTYPE), b.astype(MXU_DTYPE), preferred_element_type=F32)


def _mm_nt(a, b):
    return lax.dot_general(a.astype(MXU_DTYPE), b.astype(MXU_DTYPE), (((1,), (1,)), ((), ())),
                           preferred_element_type=F32)


def _mm_tn(a, b):
    return lax.dot_general(a.astype(MXU_DTYPE), b.astype(MXU_DTYPE), (((0,), (0,)), ((), ())),
                           preferred_element_type=F32)


def _split(x, n):
    parts, r = [], x
    for i in range(n):
        p = r.astype(BF16)
        parts.append(p)
        if i + 1 < n:
            r = r - p.astype(F32)
    return parts


def _sel_r(x, e, n):
    acc = None
    for p in _split(x, n):
        d = jnp.dot(p, e, preferred_element_type=F32)
        acc = d if acc is None else acc + d
    return acc


def _sel_l(e, x, n):
    acc = None
    for p in _split(x, n):
        d = jnp.dot(e, p, preferred_element_type=F32)
        acc = d if acc is None else acc + d
    return acc


def _iota2(shape, dim):
    return lax.broadcasted_iota(jnp.int32, shape, dim)


def _head_ones(width, head):
    r, c = _iota2((width, width), 0), _iota2((width, width), 1)
    return jnp.where((r // head) == (c // head), 1.0, 0.0).astype(BF16)


def _head_sum(x, ones_g, n):
    w = ones_g.shape[0]
    return jnp.concatenate([_sel_r(x[:, i:i + w], ones_g, n) for i in range(0, x.shape[1], w)], axis=1)


def _chunk_tril(tb, chunk):
    r, c = _iota2((tb, tb), 0), _iota2((tb, tb), 1)
    return jnp.where(((r // chunk) == (c // chunk)) & (c <= r), 1.0, 0.0).astype(BF16)


def _chunk_last(x, tb, chunk):
    return jnp.concatenate(
        [jnp.broadcast_to(x[(c + 1) * chunk - 1:(c + 1) * chunk, :], (chunk, x.shape[1]))
         for c in range(tb // chunk)], axis=0)


def _sigmoid(x):
    return 1.0 / (1.0 + jnp.exp(-x))


def _softplus(x):
    return jnp.maximum(x, 0.0) + jnp.log1p(jnp.exp(-jnp.abs(x)))


GROUP = 4
GROUP_W = GROUP * HEAD_DIM
N_GROUPS = N_HEADS_AB // GROUP


def _packed_masks():
    c = CHUNK
    r, l = _iota2((c, GROUP_W), 0), _iota2((c, GROUP_W), 1) % c
    rr, cc = _iota2((GROUP_W, GROUP_W), 0), _iota2((GROUP_W, GROUP_W), 1)
    return dict(causal=r >= l, strict=r > l, eye=jnp.where(r == l, 1.0, 0.0).astype(F32),
                same_sub=(r // SUB) == (l // SUB), blk=(rr // c) == (cc // c))


def _bd(x, blk):
    xb = x.astype(MXU_DTYPE)
    return jnp.where(blk, jnp.concatenate([xb] * GROUP, axis=0), jnp.zeros((), MXU_DTYPE))


def _pmm(a, ybd):
    return jnp.dot(a.astype(MXU_DTYPE), ybd, preferred_element_type=F32)


def _pmm_nt(a, ybd):
    return lax.dot_general(a.astype(MXU_DTYPE), ybd, (((1,), (1,)), ((), ())), preferred_element_type=F32)


def _each(f, *seqs):
    return [f(*args) for args in zip(*seqs)]


def _inv_unit_lower(a, mk):
    mm = lambda xs, ys: _each(lambda x, y: _pmm(x, _bd(y, mk["blk"])), xs, ys)
    add = lambda xs, ys: _each(jnp.add, xs, ys)
    eye_minus = lambda xs: _each(lambda x: mk["eye"] - x, xs)
    d = _each(lambda x: jnp.where(mk["same_sub"], x, 0.0), a)
    l = _each(jnp.subtract, a, d)
    d2 = mm(d, d)
    d4 = mm(d2, d2)
    d8 = mm(d4, d4)
    p = eye_minus(d)
    p = add(p, mm(p, d2))
    p = add(p, mm(p, d4))
    td = add(p, mm(p, d8))
    n = mm(td, l)
    n2 = mm(n, n)
    r = eye_minus(n)
    r = add(r, mm(r, n2))
    return mm(r, td)


def _scan_scratch(tb):
    nc = tb // CHUNK
    per = lambda rows, dt: pltpu.VMEM((nc, N_GROUPS, rows, GROUP_W), dt)
    return [pltpu.VMEM((N_GROUPS, GROUP_W, GROUP_W), F32), per(CHUNK, MXU_DTYPE), per(CHUNK, F32),
            per(GROUP_W, MXU_DTYPE), per(GROUP_W, F32), pltpu.VMEM((tb, W_AB), F32)]


def _rms(x, g):
    return x * lax.rsqrt(jnp.mean(x * x, axis=-1, keepdims=True) + RMS_EPS) * g


def _norm_proj_kernel(x_ref, g_ref, *refs, n_out):
    u = _rms(x_ref[...], g_ref[...]).astype(MXU_DTYPE)
    for w_ref, o_ref in zip(refs[:n_out], refs[n_out:]):
        o_ref[...] = jnp.dot(u, w_ref[...], preferred_element_type=F32).astype(o_ref.dtype)


def _norm_proj(x, gain, weights, out_dtypes, tm):
    m, d = x.shape
    n_out = len(weights)
    resident = lambda shape: pl.BlockSpec(shape, lambda i: (0, 0), pipeline_mode=pl.Buffered(1))
    return pl.pallas_call(
        functools.partial(_norm_proj_kernel, n_out=n_out),
        grid=(m // tm,),
        in_specs=[pl.BlockSpec((tm, d), lambda i: (i, 0)), resident((1, d))]
                 + [resident(w.shape) for w in weights],
        out_specs=[pl.BlockSpec((tm, w.shape[1]), lambda i: (i, 0)) for w in weights],
        out_shape=[jax.ShapeDtypeStruct((m, w.shape[1]), dt) for w, dt in zip(weights, out_dtypes)],
        compiler_params=pltpu.CompilerParams(dimension_semantics=("parallel",),
                                             vmem_limit_bytes=VMEM_LIMIT),
        name="norm_proj",
    )(x, gain.reshape(1, d), *weights)


def _out_proj_kernel(*refs, n_in):
    a_refs, w_refs = refs[:n_in], refs[n_in:2 * n_in]
    h_ref, g_ref, o_ref = refs[2 * n_in:]
    acc = None
    for a_ref, w_ref in zip(a_refs, w_refs):
        d = jnp.dot(a_ref[...], w_ref[...], preferred_element_type=F32)
        acc = d if acc is None else acc + d
    o_ref[...] = h_ref[...] + _rms(acc, g_ref[...])


def _out_proj(acts, weights, h, gain, tm):
    m, d = h.shape
    n_in = len(acts)
    resident = lambda shape: pl.BlockSpec(shape, lambda i: (0, 0), pipeline_mode=pl.Buffered(1))
    return pl.pallas_call(
        functools.partial(_out_proj_kernel, n_in=n_in),
        grid=(m // tm,),
        in_specs=[pl.BlockSpec((tm, a.shape[1]), lambda i: (i, 0)) for a in acts]
                 + [resident(w.shape) for w in weights]
                 + [pl.BlockSpec((tm, d), lambda i: (i, 0)), resident((1, d))],
        out_specs=pl.BlockSpec((tm, d), lambda i: (i, 0)),
        out_shape=jax.ShapeDtypeStruct((m, d), F32),
        compiler_params=pltpu.CompilerParams(dimension_semantics=("parallel",),
                                             vmem_limit_bytes=VMEM_LIMIT),
        name="out_proj",
    )(*acts, *weights, h, gain.reshape(1, d))


def _mlp_kernel(h_ref, g1_ref, wu_ref, wd_ref, g2_ref, o_ref, *, ff_blk):
    x = h_ref[...]
    u = _rms(x, g1_ref[...]).astype(MXU_DTYPE)
    acc = None
    for j in range(wu_ref.shape[1] // ff_blk):
        a = jnp.dot(u, wu_ref[:, j * ff_blk:(j + 1) * ff_blk], preferred_element_type=F32)
        a = jnp.square(jnp.maximum(a, 0.0)).astype(MXU_DTYPE)
        d = jnp.dot(a, wd_ref[j * ff_blk:(j + 1) * ff_blk, :], preferred_element_type=F32)
        acc = d if acc is None else acc + d
    o_ref[...] = x + _rms(acc, g2_ref[...])


def _mlp(h, g1, w_up, w_down, g2, tm, ff_blk):
    m, d = h.shape
    resident = lambda shape: pl.BlockSpec(shape, lambda i: (0, 0), pipeline_mode=pl.Buffered(1))
    return pl.pallas_call(
        functools.partial(_mlp_kernel, ff_blk=ff_blk),
        grid=(m // tm,),
        in_specs=[pl.BlockSpec((tm, d), lambda i: (i, 0)), resident((1, d)),
                  resident(w_up.shape), resident(w_down.shape), resident((1, d))],
        out_specs=pl.BlockSpec((tm, d), lambda i: (i, 0)),
        out_shape=jax.ShapeDtypeStruct((m, d), F32),
        compiler_params=pltpu.CompilerParams(dimension_semantics=("parallel",),
                                             vmem_limit_bytes=VMEM_LIMIT),
        name="mlp",
    )(h, g1.reshape(1, d), w_up, w_down, g2.reshape(1, d))


def _gdn_kernel(qkv_ref, z_ref, gt_ref, cw_ref, gp_ref, nw_ref, o_ref,
                xp_ref, s_ref, qeff_s, o0_s, mt_s, n0_s, o_s, *, tb):
    nh, hd, c = N_HEADS_AB, HEAD_DIM, CHUNK
    nc = tb // c

    @pl.when(pl.program_id(1) == 0)
    def _():
        s_ref[...] = jnp.zeros_like(s_ref)
        xp_ref[0:SUBLANES, :] = jnp.zeros((SUBLANES, xp_ref.shape[1]), F32)

    x = qkv_ref[...]
    xp_ref[SUBLANES:SUBLANES + tb, :] = x
    cw = cw_ref[...]
    y = x * cw[CONV_WIDTH - 1:CONV_WIDTH, :]
    for j in range(CONV_WIDTH - 1):
        y = y + xp_ref[pl.ds(SUBLANES - (CONV_WIDTH - 1) + j, tb), :] * cw[j:j + 1, :]
    xp_ref[0:SUBLANES, :] = x[tb - SUBLANES:tb, :]
    y = y * _sigmoid(y)
    q, k, v = y[:, 0:W_AB], y[:, W_AB:2 * W_AB], y[:, 2 * W_AB:3 * W_AB]

    ones_h = _head_ones(GROUP_W, hd)
    qn = q * lax.rsqrt(_head_sum(q * q, ones_h, 2) + L2_EPS) * (hd ** -0.5)
    kn = k * lax.rsqrt(_head_sum(k * k, ones_h, 2) + L2_EPS)

    gt = gt_ref[...]
    gp = gp_ref[...]
    beta = _sigmoid(gt)
    g = -jnp.exp(gp[0:1, :]) * _softplus(gt + gp[1:2, :])
    gcum = _sel_l(_chunk_tril(tb, c), g, 3)
    r_, c_ = _iota2((LANES, W_AB), 0), _iota2((LANES, W_AB), 1)
    pick_beta = jnp.where(r_ == c_ // hd, 1.0, 0.0).astype(BF16)
    pick_g = jnp.where(r_ == nh + c_ // hd, 1.0, 0.0).astype(BF16)
    beta_e = _sel_r(beta, pick_beta, 2)
    gc_e = _sel_r(gcum, pick_g, 3)
    glast_e = _chunk_last(gc_e, tb, c)
    gct = gcum.T

    eg = jnp.exp(gc_e)
    kb = kn * beta_e
    kbe, vb, qd = kb * eg, v * beta_e, qn * eg
    kd = kn * jnp.exp(glast_e - gc_e)
    blk_decay = jnp.exp(glast_e)

    mk = _packed_masks()
    blk = mk["blk"]

    probs = [(ci, gi) for ci in range(nc) for gi in range(N_GROUPS)]
    tile = lambda x: [x[ci * c:(ci + 1) * c, gi * GROUP_W:(gi + 1) * GROUP_W] for ci, gi in probs]
    g_rows = [jnp.concatenate([gct[nh + h:nh + h + 1, ci * c:(ci + 1) * c] for h in range(nh)], axis=1)
              for ci in range(nc)]
    diff = _each(lambda gcol, p: gcol - g_rows[p[0]][:, p[1] * GROUP_W:(p[1] + 1) * GROUP_W], tile(gc_e), probs)
    dec = _each(lambda x: jnp.where(mk["causal"], jnp.exp(jnp.where(mk["causal"], x, 0.0)), 0.0), diff)
    both = _each(lambda kb_, q_, k_: _pmm_nt(jnp.concatenate([kb_, q_], axis=0), _bd(k_, blk)),
                 tile(kb), tile(qn), tile(kn))
    a = _each(lambda x, e: jnp.where(mk["strict"], x[:c] * e, 0.0), both, dec)
    attn = _each(lambda x, e: x[c:] * e, both, dec)
    t = _inv_unit_lower(a, mk)
    w = _each(lambda t_, x: _pmm(t_, _bd(x, blk)), t, tile(kbe))
    u = _each(lambda t_, x: _pmm(t_, _bd(x, blk)), t, tile(vb))
    aw = _each(lambda at, x: _pmm(at, _bd(x, blk)), attn, w)
    o0 = _each(lambda at, x: _pmm(at, _bd(x, blk)), attn, u)
    mt = _each(lambda kd_, x: jnp.where(blk, _mm_tn(kd_, x), 0.0), tile(kd), w)
    n0 = _each(lambda kd_, x: jnp.where(blk, _mm_tn(kd_, x), 0.0), tile(kd), u)
    for (ci, gi), qd_, aw_, o0_, mt_, n0_ in zip(probs, tile(qd), aw, o0, mt, n0):
        qeff_s[ci, gi] = (qd_ - aw_).astype(qeff_s.dtype)
        o0_s[ci, gi] = o0_
        mt_s[ci, gi] = mt_.astype(mt_s.dtype)
        n0_s[ci, gi] = n0_

    for ci in range(nc):
        rs = slice(ci * c, (ci + 1) * c)
        for gi in range(N_GROUPS):
            ls = slice(gi * GROUP_W, (gi + 1) * GROUP_W)
            s = s_ref[gi]
            sb = s.astype(MXU_DTYPE)
            o_s[rs, ls] = jnp.dot(qeff_s[ci, gi], sb, preferred_element_type=F32) + o0_s[ci, gi]
            s_ref[gi] = (s * blk_decay[ci * c:ci * c + 1, ls]
                         - jnp.dot(mt_s[ci, gi], sb, preferred_element_type=F32) + n0_s[ci, gi])

    o = o_s[...]
    o = o * lax.rsqrt(_head_sum(o * o, ones_h, 2) * (1.0 / hd) + RMS_EPS) * nw_ref[...]
    z = z_ref[...]
    o_ref[...] = (o * (z * _sigmoid(z))).astype(o_ref.dtype)


def _gdn(qkv, z, gates, conv_w, gate_params, norm_w, batch, tb):
    m = qkv.shape[0]
    nt = m // batch // tb
    row_blk = lambda n: pl.BlockSpec((tb, n), lambda b, t: (b * nt + t, 0))
    const = lambda shape: pl.BlockSpec(shape, lambda b, t: (0, 0))
    return pl.pallas_call(
        functools.partial(_gdn_kernel, tb=tb),
        grid=(batch, nt),
        in_specs=[row_blk(3 * W_AB), row_blk(W_AB), row_blk(LANES),
                  const(conv_w.shape), const(gate_params.shape), const(norm_w.shape)],
        out_specs=row_blk(W_AB),
        out_shape=jax.ShapeDtypeStruct((m, W_AB), BF16),
        scratch_shapes=[pltpu.VMEM((SUBLANES + tb, 3 * W_AB), F32)] + _scan_scratch(tb),
        compiler_params=pltpu.CompilerParams(dimension_semantics=("parallel", "arbitrary"),
                                             vmem_limit_bytes=VMEM_LIMIT),
        name="gdn",
    )(qkv, z, gates, conv_w, gate_params, norm_w)


def _rwkv_kernel(rp_ref, mu_ref, w0_ref, w2_ref, a0_ref, a2_ref, g2_ref, kk_ref, ka_ref, rk_ref,
                 lnw_ref, lnb_ref, o_ref,
                 xp_ref, s_ref, qeff_s, y0_s, mt_s, n0_s, y_s, *, tb):
    nh, hd, c = N_HEADS_AB, HEAD_DIM, CHUNK
    nc = tb // c

    @pl.when(pl.program_id(1) == 0)
    def _():
        s_ref[...] = jnp.zeros_like(s_ref)
        xp_ref[0:SUBLANES, :] = jnp.zeros((SUBLANES, xp_ref.shape[1]), F32)

    rp = rp_ref[...]
    xp_ref[SUBLANES:SUBLANES + tb, :] = rp
    prev = xp_ref[pl.ds(SUBLANES - 1, tb), :]
    xp_ref[0:SUBLANES, :] = rp[tb - SUBLANES:tb, :]
    xs = rp + (prev - rp) * mu_ref[...]
    r, kr, vr = xs[:, 0:W_AB], xs[:, W_AB:2 * W_AB], xs[:, 2 * W_AB:3 * W_AB]
    xwa = xs[:, 3 * W_AB:3 * W_AB + 2 * RWKV_LORA]
    xg = xs[:, 3 * W_AB + 2 * RWKV_LORA:]

    w_log = -_softplus(-(w0_ref[...] + _mm(jnp.tanh(xwa), w2_ref[...]))) - 0.5
    lw = -jnp.exp(w_log)
    a = _sigmoid(a0_ref[...] + _mm(xwa, a2_ref[...]))
    gate = _mm(_sigmoid(xg), g2_ref[...])

    ones_h = _head_ones(GROUP_W, hd)
    kk = kr * kk_ref[...]
    kk = kk * lax.rsqrt(_head_sum(kk * kk, ones_h, 2) + L2_EPS)
    k = kr * (1.0 + (a - 1.0) * ka_ref[...])
    b = kk * a

    cum = _sel_l(_chunk_tril(tb, c), lw, 3)
    last = _chunk_last(cum, tb, c)
    e_neg = jnp.exp(-cum)
    e_rem = jnp.exp(last - cum)
    rh_all, kkh_all = r * jnp.exp(cum), kk * jnp.exp(cum - lw)
    bh_all, kh_all, bt_all, kt_all = b * e_neg, k * e_neg, b * e_rem, k * e_rem
    chunk_decay = jnp.exp(last)

    mk = _packed_masks()
    blk, causal, strict = mk["blk"], mk["causal"], mk["strict"]

    probs = [(ci, gi) for ci in range(nc) for gi in range(N_GROUPS)]
    tile = lambda x: [x[ci * c:(ci + 1) * c, gi * GROUP_W:(gi + 1) * GROUP_W] for ci, gi in probs]
    rh, kkh, vv, bt, kt = tile(rh_all), tile(kkh_all), tile(vr), tile(bt_all), tile(kt_all)
    lhs = _each(lambda x, y: jnp.concatenate([x, y], axis=0), kkh, rh)
    ab = _each(lambda x, y: _pmm_nt(x, _bd(y, blk)), lhs, tile(bh_all))
    ak = _each(lambda x, y: _pmm_nt(x, _bd(y, blk)), lhs, tile(kh_all))
    a_bb = _each(lambda x: jnp.where(strict, x[:c], 0.0), ab)
    a_rb = _each(lambda x: jnp.where(causal, x[c:], 0.0), ab)
    a_k = _each(lambda x: jnp.concatenate([jnp.where(strict, x[:c], 0.0), jnp.where(causal, x[c:], 0.0)],
                                          axis=0), ak)
    t = _inv_unit_lower(a_bb, mk)
    z0 = _each(lambda x, y: _pmm(x, _bd(y, blk)), a_k, vv)
    wk = _each(lambda x, y: _pmm(x, _bd(y, blk)), t, kkh)
    u0 = _each(lambda x, y: -_pmm(x, _bd(y[:c], blk)), t, z0)
    ar = _each(lambda x, y: _pmm(x, _bd(y, blk)), a_rb, wk)
    y0 = _each(lambda x, y, z: _pmm(x, _bd(y, blk)) + z[c:], a_rb, u0, z0)
    mt = _each(lambda x, y: jnp.where(blk, _mm_tn(x, y), 0.0), wk, bt)
    n0 = _each(lambda u_, v_, b_, k_: jnp.where(blk, _mm_tn(jnp.concatenate([u_, v_], axis=0),
                                                            jnp.concatenate([b_, k_], axis=0)), 0.0),
               u0, vv, bt, kt)
    for (ci, gi), rh_, ar_, y0_, mt_, n0_ in zip(probs, rh, ar, y0, mt, n0):
        qeff_s[ci, gi] = (rh_ - ar_).astype(qeff_s.dtype)
        y0_s[ci, gi] = y0_
        mt_s[ci, gi] = mt_.astype(mt_s.dtype)
        n0_s[ci, gi] = n0_

    for ci in range(nc):
        rs = slice(ci * c, (ci + 1) * c)
        for gi in range(N_GROUPS):
            ls = slice(gi * GROUP_W, (gi + 1) * GROUP_W)
            s = s_ref[gi]
            sb = s.astype(MXU_DTYPE)
            y_s[rs, ls] = lax.dot_general(qeff_s[ci, gi], sb, (((1,), (1,)), ((), ())),
                                          preferred_element_type=F32) + y0_s[ci, gi]
            s_ref[gi] = (s * chunk_decay[ci * c:ci * c + 1, ls]
                         - jnp.dot(sb, mt_s[ci, gi], preferred_element_type=F32) + n0_s[ci, gi])

    y = y_s[...]
    inv_hd = 1.0 / hd
    yc = y - _head_sum(y, ones_h, 2) * inv_hd
    yn = yc * lax.rsqrt(_head_sum(yc * yc, ones_h, 2) * inv_hd + RWKV_GN_EPS) * lnw_ref[...] + lnb_ref[...]
    bonus = _head_sum(r * k * rk_ref[...], ones_h, 2) * vr
    o_ref[...] = ((yn + bonus) * gate).astype(o_ref.dtype)


def _rwkv(rp, params, batch, tb):
    m, n_in = rp.shape
    nt = m // batch // tb
    row_blk = lambda n: pl.BlockSpec((tb, n), lambda b, t: (b * nt + t, 0))
    const = lambda shape: pl.BlockSpec(shape, lambda b, t: (0, 0))
    return pl.pallas_call(
        functools.partial(_rwkv_kernel, tb=tb),
        grid=(batch, nt),
        in_specs=[row_blk(n_in)] + [const(p.shape) for p in params],
        out_specs=row_blk(W_AB),
        out_shape=jax.ShapeDtypeStruct((m, W_AB), BF16),
        scratch_shapes=[pltpu.VMEM((SUBLANES + tb, n_in), F32)] + _scan_scratch(tb),
        compiler_params=pltpu.CompilerParams(dimension_semantics=("parallel", "arbitrary"),
                                             vmem_limit_bytes=VMEM_LIMIT),
        name="rwkv7",
    )(rp, *params)


def _rope_table_kernel(angle_ref, cos_ref, sin_ref):
    shape = cos_ref.shape
    pos = _iota2(shape, 0).astype(F32)
    theta = pos * angle_ref[...]
    cos_ref[...] = jnp.cos(theta)
    s = jnp.sin(theta)
    sin_ref[...] = jnp.where(_iota2(shape, 1) < RET_DK // 2, -s, s)


def _rope_tables(seq):
    angle = 1.0 / (ROPE_BASE ** jnp.linspace(0.0, 1.0, RET_DK // 2, dtype=F32))
    angle = jnp.concatenate([angle, angle]).reshape(1, RET_DK)
    return pl.pallas_call(
        _rope_table_kernel,
        out_shape=[jax.ShapeDtypeStruct((seq, RET_DK), F32)] * 2,
        name="rope_tables",
    )(angle)


def _ret_log_gamma():
    return [float(np.log1p(-np.exp2(np.float32(-5.0 - h)))) for h in range(N_HEADS_RET)]


def _ret_kernel(q_ref, k_ref, v_ref, g_ref, cos_ref, sin_ref, gnw_ref, o_ref, s_ref, *, tb):
    dk, dv = RET_DK, RET_DV

    @pl.when(pl.program_id(1) == 0)
    def _():
        s_ref[...] = jnp.zeros_like(s_ref)

    cos, sin = cos_ref[...], sin_ref[...]
    ri, cj = _iota2((tb, tb), 0), _iota2((tb, tb), 1)
    rel = (ri - cj).astype(F32)
    idx = _iota2((tb, 1), 0).astype(F32)
    for h, lg in enumerate(_ret_log_gamma()):
        q = q_ref[:, h * dk:(h + 1) * dk]
        k = k_ref[:, h * dk:(h + 1) * dk]
        q = q * cos + pltpu.roll(q, dk // 2, 1) * sin
        k = (k * cos + pltpu.roll(k, dk // 2, 1) * sin) * (dk ** -0.5)
        v = v_ref[:, h * dv:(h + 1) * dv]
        intra = jnp.where(rel >= 0.0, jnp.exp(lg * jnp.maximum(rel, 0.0)), 0.0)
        scores = _mm_nt(q, k) * intra
        s = s_ref[h]
        y = _mm(scores, v) + _mm(q * jnp.exp(lg * (idx + 1.0)), s)
        s_ref[h] = s * float(np.exp(lg * tb)) + _mm_tn(k * jnp.exp(lg * (tb - 1.0 - idx)), v)
        yc = y - jnp.mean(y, axis=-1, keepdims=True)
        yn = yc * lax.rsqrt(jnp.mean(yc * yc, axis=-1, keepdims=True) + RET_GN_EPS)
        g = g_ref[:, h * dv:(h + 1) * dv]
        o_ref[:, h * dv:(h + 1) * dv] = (g * _sigmoid(g) * yn * gnw_ref[:, h * dv:(h + 1) * dv]
                                         ).astype(o_ref.dtype)


def _retention(q, k, v, g, cos, sin, gn_w, batch, tb):
    m = q.shape[0]
    nt = m // batch // tb
    row_blk = lambda n: pl.BlockSpec((tb, n), lambda b, t: (b * nt + t, 0))
    pos_blk = pl.BlockSpec((tb, RET_DK), lambda b, t: (t, 0))
    wv = N_HEADS_RET * RET_DV
    return pl.pallas_call(
        functools.partial(_ret_kernel, tb=tb),
        grid=(batch, nt),
        in_specs=[row_blk(q.shape[1]), row_blk(k.shape[1]), row_blk(wv), row_blk(wv), pos_blk, pos_blk,
                  pl.BlockSpec((1, wv), lambda b, t: (0, 0))],
        out_specs=row_blk(wv),
        out_shape=jax.ShapeDtypeStruct((m, wv), BF16),
        scratch_shapes=[pltpu.VMEM((N_HEADS_RET, RET_DK, RET_DV), F32)],
        compiler_params=pltpu.CompilerParams(dimension_semantics=("parallel", "arbitrary"),
                                             vmem_limit_bytes=VMEM_LIMIT),
        name="retention",
    )(q, k, v, g, cos, sin, gn_w)


def _tiles(m, seq):
    tm = 512 if m % 512 == 0 else 256
    tb_ab = 256 if seq % 256 == 0 else CHUNK
    tb_ret = 128
    return tm, tb_ab, tb_ret


def _row(v):
    return v.reshape(1, -1).astype(F32)


def _even_layer_mix(h, batch, seq, pre_gain, post_gain, w_in, conv_w, a_log, dt_bias, gdn_norm_w,
                    mu, w0, w2, a0, a2, g2, k_k, k_a, r_k, ln_w, ln_b, w_out):
    tm, tb, _ = _tiles(h.shape[0], seq)
    nh = N_HEADS_AB
    w_in = w_in.astype(MXU_DTYPE)
    gate_cols = jnp.pad(w_in[:, 4 * W_AB:4 * W_AB + 2 * nh], ((0, 0), (0, LANES - 2 * nh)))
    weights = [w_in[:, :3 * W_AB], w_in[:, 3 * W_AB:4 * W_AB], gate_cols, w_in[:, 4 * W_AB + 2 * nh:]]
    qkv, z, gates, rp = _norm_proj(h, pre_gain, weights, [F32] * 4, tm)

    pad_g = lambda v: jnp.pad(v.astype(F32), (nh, LANES - 2 * nh))
    gate_params = jnp.stack([pad_g(a_log), pad_g(dt_bias)])
    o_a = _gdn(qkv, z, gates, conv_w.astype(F32), gate_params,
               _row(jnp.tile(gdn_norm_w, nh)), batch, tb)

    zeros = jnp.zeros((RWKV_LORA, W_AB), F32)
    params = [_row(mu), _row(w0), jnp.concatenate([w2, zeros]).astype(MXU_DTYPE), _row(a0),
              jnp.concatenate([zeros, a2]).astype(MXU_DTYPE), g2.astype(MXU_DTYPE),
              _row(k_k), _row(k_a), _row(r_k), _row(ln_w), _row(ln_b)]
    o_b = _rwkv(rp, params, batch, tb)

    w_out = w_out.astype(MXU_DTYPE)
    return _out_proj([o_a, o_b], [w_out[:W_AB], w_out[W_AB:]], h, post_gain, tm)


def _odd_layer_mix(h, batch, seq, pre_gain, post_gain, w_in, gn_w, w_out):
    tm, _, tb = _tiles(h.shape[0], seq)
    d = h.shape[1]
    wv = N_HEADS_RET * RET_DV
    perm = np.concatenate([np.concatenate([np.arange(0, RET_DK, 2), np.arange(1, RET_DK, 2)]) + hh * RET_DK
                           for hh in range(N_HEADS_RET)])
    w_in = w_in.astype(MXU_DTYPE)
    weights = [w_in[:, :d][:, perm], w_in[:, d:2 * d][:, perm], w_in[:, 2 * d:2 * d + wv], w_in[:, 2 * d + wv:]]
    q, k, v, g = _norm_proj(h, pre_gain, weights, [F32] * 4, tm // 2)
    cos, sin = _rope_tables(seq)
    y = _retention(q, k, v, g, cos, sin, _row(gn_w), batch, tb)
    return _out_proj([y], [w_out.astype(MXU_DTYPE)], h, post_gain, tm)


def kernel(x, norm_mix_pre, norm_mix_post, norm_mlp_pre, norm_mlp_post, mlp_w_up, mlp_w_down, ab_w_in, gdn_conv_w, gdn_a_log, gdn_dt_bias, gdn_norm_w, rwkv_mu, rwkv_w0, rwkv_w2, rwkv_a0, rwkv_a2, rwkv_g2, rwkv_k_k, rwkv_k_a, rwkv_r_k, rwkv_ln_w, rwkv_ln_b, ab_w_out, ret_w_in, ret_gn_w, ret_w_out):
    batch, seq, d = x.shape
    h = x.astype(F32).reshape(batch * seq, d)
    tm = _tiles(batch * seq, seq)[0]
    for layer in range(norm_mix_pre.shape[0]):
        j = layer // 2
        if layer % 2 == 0:
            h = _even_layer_mix(h, batch, seq, norm_mix_pre[layer], norm_mix_post[layer], ab_w_in[j],
                                gdn_conv_w[j], gdn_a_log[j], gdn_dt_bias[j], gdn_norm_w[j], rwkv_mu[j],
                                rwkv_w0[j], rwkv_w2[j], rwkv_a0[j], rwkv_a2[j], rwkv_g2[j], rwkv_k_k[j],
                                rwkv_k_a[j], rwkv_r_k[j], rwkv_ln_w[j], rwkv_ln_b[j], ab_w_out[j])
        else:
            h = _odd_layer_mix(h, batch, seq, norm_mix_pre[layer], norm_mix_post[layer], ret_w_in[j],
                               ret_gn_w[j], ret_w_out[j])
        h = _mlp(h, norm_mlp_pre[layer], mlp_w_up[layer].astype(MXU_DTYPE),
                 mlp_w_down[layer].astype(MXU_DTYPE), norm_mlp_post[layer], tm, 1024)
    return h.reshape(batch, seq, d).astype(x.dtype)
```

```python
import functools

import numpy as np
import jax
import jax.numpy as jnp
from jax import lax
from jax.experimental import pallas as pl
from jax.experimental.pallas import tpu as pltpu

F32 = jnp.float32
BF16 = jnp.bfloat16
MXU_DTYPE = BF16

RMS_EPS = 1e-6
L2_EPS = 1e-6
HEAD_DIM = 64
N_HEADS_AB = 8
W_AB = N_HEADS_AB * HEAD_DIM
CHUNK = 64
SUB = 16
CONV_WIDTH = 4
RWKV_GN_EPS = 64e-5
RWKV_LORA = 64
RWKV_GATE_LORA = 128
N_HEADS_RET = 8
RET_DK = 128
RET_DV = 256
RET_CHUNK = 128
RET_GN_EPS = 1e-6
ROPE_BASE = 10000.0
LANES = 128
SUBLANES = 8
VMEM_LIMIT = 56 * 1024 * 1024


def _mm(a, b):
    return jnp.dot(a.astype(MXU_DTYPE), b.astype(MXU_DTYPE), preferred_element_type=F32)


def _mm_nt(a, b):
    return lax.dot_general(a.astype(MXU_DTYPE), b.astype(MXU_DTYPE), (((1,), (1,)), ((), ())),
                           preferred_element_type=F32)


def _mm_tn(a, b):
    return lax.dot_general(a.astype(MXU_DTYPE), b.astype(MXU_DTYPE), (((0,), (0,)), ((), ())),
                           preferred_element_type=F32)


def _split(x, n):
    parts, r = [], x
    for i in range(n):
        p = r.astype(BF16)
        parts.append(p)
        if i + 1 < n:
            r = r - p.astype(F32)
    return parts


def _sel_r(x, e, n):
    acc = None
    for p in _split(x, n):
        d = jnp.dot(p, e, preferred_element_type=F32)
        acc = d if acc is None else acc + d
    return acc


def _sel_l(e, x, n):
    acc = None
    for p in _split(x, n):
        d = jnp.dot(e, p, preferred_element_type=F32)
        acc = d if acc is None else acc + d
    return acc


def _iota2(shape, dim):
    return lax.broadcasted_iota(jnp.int32, shape, dim)


def _head_ones(width, head):
    r, c = _iota2((width, width), 0), _iota2((width, width), 1)
    return jnp.where((r // head) == (c // head), 1.0, 0.0).astype(BF16)


def _head_sum(x, ones_g, n):
    w = ones_g.shape[0]
    return jnp.concatenate([_sel_r(x[:, i:i + w], ones_g, n) for i in range(0, x.shape[1], w)], axis=1)


def _chunk_tril(tb, chunk):
    r, c = _iota2((tb, tb), 0), _iota2((tb, tb), 1)
    return jnp.where(((r // chunk) == (c // chunk)) & (c <= r), 1.0, 0.0).astype(BF16)


def _chunk_last(x, tb, chunk):
    return jnp.concatenate(
        [jnp.broadcast_to(x[(c + 1) * chunk - 1:(c + 1) * chunk, :], (chunk, x.shape[1]))
         for c in range(tb // chunk)], axis=0)


def _sigmoid(x):
    return 1.0 / (1.0 + jnp.exp(-x))


def _softplus(x):
    return jnp.maximum(x, 0.0) + jnp.log1p(jnp.exp(-jnp.abs(x)))


GROUP = 4
GROUP_W = GROUP * HEAD_DIM
N_GROUPS = N_HEADS_AB // GROUP


def _packed_masks():
    c = CHUNK
    r, l = _iota2((c, GROUP_W), 0), _iota2((c, GROUP_W), 1) % c
    rr, cc = _iota2((GROUP_W, GROUP_W), 0), _iota2((GROUP_W, GROUP_W), 1)
    return dict(causal=r >= l, strict=r > l, eye=jnp.where(r == l, 1.0, 0.0).astype(F32),
                same_sub=(r // SUB) == (l // SUB), blk=(rr // c) == (cc // c))


def _bd(x, blk):
    xb = x.astype(MXU_DTYPE)
    return jnp.where(blk, jnp.concatenate([xb] * GROUP, axis=0), jnp.zeros((), MXU_DTYPE))


def _pmm(a, ybd):
    return jnp.dot(a.astype(MXU_DTYPE), ybd, preferred_element_type=F32)


def _pmm_nt(a, ybd):
    return lax.dot_general(a.astype(MXU_DTYPE), ybd, (((1,), (1,)), ((), ())), preferred_element_type=F32)


def _each(f, *seqs):
    return [f(*args) for args in zip(*seqs)]


class _Interleave:
    def __init__(self, items):
        self._items = list(items)

    def __call__(self, n=1):
        for _ in range(min(n, len(self._items))):
            self._items.pop(0)()

    def flush(self):
        self(len(self._items))


def _project_ahead(h_ref, gain_ref, w_ref, u_s, p_s):
    u_s[...] = _rms(h_ref[...], gain_ref[...]).astype(u_s.dtype)

    def item(lo, hi):
        def run():
            p_s[:, lo:hi] = jnp.dot(u_s[...], w_ref[:, lo:hi], preferred_element_type=F32)
        return run

    n = w_ref.shape[1]
    return _Interleave([item(lo, min(lo + GROUP_W, n)) for lo in range(0, n, GROUP_W)])


def _lookahead_specs(n_blocks, tb, d):
    h_spec = pl.BlockSpec((tb, d), lambda i: (jnp.minimum(i, n_blocks - 1), 0))
    out_spec = lambda n: pl.BlockSpec((tb, n), lambda i: (jnp.maximum(i - 1, 0), 0))
    const = lambda shape: pl.BlockSpec(shape, lambda i: (0,) * len(shape), pipeline_mode=pl.Buffered(1))
    return h_spec, out_spec, const


def _starts_sequence(nt):
    i = pl.program_id(0)
    return (i == 0) | (lax.rem(i + nt - 1, nt) == 0)


def _inv_unit_lower(a, mk, fill):
    c = CHUNK

    def mm(xs, ys):
        out = _each(lambda x, y: _pmm(x, _bd(y, mk["blk"])), xs, ys)
        fill()
        return out

    def mm2(xs, zs, ys):
        out = mm(_each(lambda x, z: jnp.concatenate([x, z], axis=0), xs, zs), ys)
        return [o[:c] for o in out], [o[c:] for o in out]

    add = lambda xs, ys: _each(jnp.add, xs, ys)
    d = _each(lambda x: jnp.where(mk["same_sub"], x, 0.0), a)
    l = _each(jnp.subtract, a, d)
    d2 = mm(d, d)
    p = _each(lambda x: mk["eye"] - x, d)
    pd, d4 = mm2(p, d2, d2)
    p = add(p, pd)
    pd, d8 = mm2(p, d4, d4)
    p = add(p, pd)
    td = add(p, mm(p, d8))
    n = mm(l, td)
    n2, tn = mm2(n, td, n)
    y = _each(jnp.subtract, td, tn)
    return add(y, mm(y, n2))


def _scan_scratch(tb):
    nc = tb // CHUNK
    per = lambda rows, dt: pltpu.VMEM((nc, N_GROUPS, rows, GROUP_W), dt)
    return [pltpu.VMEM((N_GROUPS, GROUP_W, GROUP_W), F32), per(CHUNK, MXU_DTYPE), per(CHUNK, F32),
            per(GROUP_W, MXU_DTYPE), per(GROUP_W, F32), pltpu.VMEM((tb, W_AB), F32)]


def _rms(x, g):
    return x * lax.rsqrt(jnp.mean(x * x, axis=-1, keepdims=True) + RMS_EPS) * g


def _norm_proj_kernel(x_ref, g_ref, *refs, n_out):
    u = _rms(x_ref[...], g_ref[...]).astype(MXU_DTYPE)
    for w_ref, o_ref in zip(refs[:n_out], refs[n_out:]):
        o_ref[...] = jnp.dot(u, w_ref[...], preferred_element_type=F32).astype(o_ref.dtype)


def _norm_proj(x, gain, weights, out_dtypes, tm):
    m, d = x.shape
    n_out = len(weights)
    resident = lambda shape: pl.BlockSpec(shape, lambda i: (0, 0), pipeline_mode=pl.Buffered(1))
    return pl.pallas_call(
        functools.partial(_norm_proj_kernel, n_out=n_out),
        grid=(m // tm,),
        in_specs=[pl.BlockSpec((tm, d), lambda i: (i, 0)), resident((1, d))]
                 + [resident(w.shape) for w in weights],
        out_specs=[pl.BlockSpec((tm, w.shape[1]), lambda i: (i, 0)) for w in weights],
        out_shape=[jax.ShapeDtypeStruct((m, w.shape[1]), dt) for w, dt in zip(weights, out_dtypes)],
        compiler_params=pltpu.CompilerParams(dimension_semantics=("parallel",),
                                             vmem_limit_bytes=VMEM_LIMIT),
        name="norm_proj",
    )(x, gain.reshape(1, d), *weights)


def _out_proj_kernel(*refs, n_in):
    a_refs, w_refs = refs[:n_in], refs[n_in:2 * n_in]
    h_ref, g_ref, o_ref = refs[2 * n_in:]
    acc = None
    for a_ref, w_ref in zip(a_refs, w_refs):
        d = jnp.dot(a_ref[...], w_ref[...], preferred_element_type=F32)
        acc = d if acc is None else acc + d
    o_ref[...] = h_ref[...] + _rms(acc, g_ref[...])


def _out_proj(acts, weights, h, gain, tm):
    m, d = h.shape
    n_in = len(acts)
    resident = lambda shape: pl.BlockSpec(shape, lambda i: (0, 0), pipeline_mode=pl.Buffered(1))
    return pl.pallas_call(
        functools.partial(_out_proj_kernel, n_in=n_in),
        grid=(m // tm,),
        in_specs=[pl.BlockSpec((tm, a.shape[1]), lambda i: (i, 0)) for a in acts]
                 + [resident(w.shape) for w in weights]
                 + [pl.BlockSpec((tm, d), lambda i: (i, 0)), resident((1, d))],
        out_specs=pl.BlockSpec((tm, d), lambda i: (i, 0)),
        out_shape=jax.ShapeDtypeStruct((m, d), F32),
        compiler_params=pltpu.CompilerParams(dimension_semantics=("parallel",),
                                             vmem_limit_bytes=VMEM_LIMIT),
        name="out_proj",
    )(*acts, *weights, h, gain.reshape(1, d))


def _mlp_kernel(h_ref, g1_ref, wu_ref, wd_ref, g2_ref, o_ref, *, ff_blk):
    x = h_ref[...]
    u = _rms(x, g1_ref[...]).astype(MXU_DTYPE)
    acc = None
    for j in range(wu_ref.shape[1] // ff_blk):
        a = jnp.dot(u, wu_ref[:, j * ff_blk:(j + 1) * ff_blk], preferred_element_type=F32)
        a = jnp.square(jnp.maximum(a, 0.0)).astype(MXU_DTYPE)
        d = jnp.dot(a, wd_ref[j * ff_blk:(j + 1) * ff_blk, :], preferred_element_type=F32)
        acc = d if acc is None else acc + d
    o_ref[...] = x + _rms(acc, g2_ref[...])


def _mlp(h, g1, w_up, w_down, g2, tm, ff_blk):
    m, d = h.shape
    resident = lambda shape: pl.BlockSpec(shape, lambda i: (0, 0), pipeline_mode=pl.Buffered(1))
    return pl.pallas_call(
        functools.partial(_mlp_kernel, ff_blk=ff_blk),
        grid=(m // tm,),
        in_specs=[pl.BlockSpec((tm, d), lambda i: (i, 0)), resident((1, d)),
                  resident(w_up.shape), resident(w_down.shape), resident((1, d))],
        out_specs=pl.BlockSpec((tm, d), lambda i: (i, 0)),
        out_shape=jax.ShapeDtypeStruct((m, d), F32),
        compiler_params=pltpu.CompilerParams(dimension_semantics=("parallel",),
                                             vmem_limit_bytes=VMEM_LIMIT),
        name="mlp",
    )(h, g1.reshape(1, d), w_up, w_down, g2.reshape(1, d))


def _gdn_kernel(h_ref, gain_ref, w_ref, cw_ref, gp_ref, nw_ref, o_ref,
                p_s, u_s, z_s, xp_ref, s_ref, qeff_s, o0_s, mt_s, n0_s, o_s, *, tb, nt):
    nh, hd, c = N_HEADS_AB, HEAD_DIM, CHUNK
    nc = tb // c

    @pl.when(pl.program_id(0) == 0)
    def _():
        p_s[...] = jnp.zeros_like(p_s)

    @pl.when(_starts_sequence(nt))
    def _():
        s_ref[...] = jnp.zeros_like(s_ref)
        xp_ref[0:SUBLANES, :] = jnp.zeros((SUBLANES, xp_ref.shape[1]), F32)

    xp_ref[SUBLANES:SUBLANES + tb, :] = p_s[:, 0:3 * W_AB]
    z_s[...] = p_s[:, 3 * W_AB:4 * W_AB]
    gt = p_s[:, 4 * W_AB:]
    fill = _project_ahead(h_ref, gain_ref, w_ref, u_s, p_s)

    cw = cw_ref[...]
    ones_h = _head_ones(GROUP_W, hd)
    slabs = []
    for lo in range(0, 3 * W_AB, GROUP_W):
        cols = slice(lo, lo + GROUP_W)
        y = xp_ref[SUBLANES:SUBLANES + tb, cols] * cw[CONV_WIDTH - 1:CONV_WIDTH, cols]
        for j in range(CONV_WIDTH - 1):
            y = y + xp_ref[pl.ds(SUBLANES - (CONV_WIDTH - 1) + j, tb), cols] * cw[j:j + 1, cols]
        y = y * _sigmoid(y)
        if lo < 2 * W_AB:
            y = y * lax.rsqrt(_sel_r(y * y, ones_h, 1) + L2_EPS)
        slabs.append(y)
        fill()
    xp_ref[0:SUBLANES, :] = xp_ref[tb:tb + SUBLANES, :]
    per = W_AB // GROUP_W
    qn = jnp.concatenate(slabs[0:per], axis=1) * (hd ** -0.5)
    kn = jnp.concatenate(slabs[per:2 * per], axis=1)
    v = jnp.concatenate(slabs[2 * per:], axis=1)

    gp = gp_ref[...]
    beta = _sigmoid(gt)
    g = -jnp.exp(gp[0:1, :]) * _softplus(gt + gp[1:2, :])
    gcum = _sel_l(_chunk_tril(tb, c), g, 3)
    r_, c_ = _iota2((LANES, W_AB), 0), _iota2((LANES, W_AB), 1)
    pick_beta = jnp.where(r_ == c_ // hd, 1.0, 0.0).astype(BF16)
    pick_g = jnp.where(r_ == nh + c_ // hd, 1.0, 0.0).astype(BF16)
    beta_e = _sel_r(beta, pick_beta, 2)
    gc_e = _sel_r(gcum, pick_g, 3)
    glast_e = _chunk_last(gc_e, tb, c)
    gct = gcum.T
    fill()

    eg = jnp.exp(gc_e)
    kb = kn * beta_e
    kbe, vb, qd = kb * eg, v * beta_e, qn * eg
    fill()
    kd = kn * jnp.exp(glast_e - gc_e)
    blk_decay = jnp.exp(glast_e)
    fill()

    mk = _packed_masks()
    blk = mk["blk"]

    probs = [(ci, gi) for ci in range(nc) for gi in range(N_GROUPS)]
    tile = lambda x: [x[ci * c:(ci + 1) * c, gi * GROUP_W:(gi + 1) * GROUP_W] for ci, gi in probs]
    g_rows = [jnp.concatenate([gct[nh + h:nh + h + 1, ci * c:(ci + 1) * c] for h in range(nh)], axis=1)
              for ci in range(nc)]
    diff = _each(lambda gcol, p: gcol - g_rows[p[0]][:, p[1] * GROUP_W:(p[1] + 1) * GROUP_W], tile(gc_e), probs)
    dec = _each(lambda x: jnp.where(mk["causal"], jnp.exp(jnp.where(mk["causal"], x, 0.0)), 0.0), diff)
    both = _each(lambda kb_, q_, k_: _pmm_nt(jnp.concatenate([kb_, q_], axis=0), _bd(k_, blk)),
                 tile(kb), tile(qn), tile(kn))
    a = _each(lambda x, e: jnp.where(mk["strict"], x[:c] * e, 0.0), both, dec)
    attn = _each(lambda x, e: x[c:] * e, both, dec)
    t = _inv_unit_lower(a, mk, fill)
    fill.flush()
    w = _each(lambda t_, x: _pmm(t_, _bd(x, blk)), t, tile(kbe))
    u = _each(lambda t_, x: _pmm(t_, _bd(x, blk)), t, tile(vb))
    aw = _each(lambda at, x: _pmm(at, _bd(x, blk)), attn, w)
    o0 = _each(lambda at, x: _pmm(at, _bd(x, blk)), attn, u)
    mt = _each(lambda kd_, x: jnp.where(blk, _mm_tn(kd_, x), 0.0), tile(kd), w)
    n0 = _each(lambda kd_, x: jnp.where(blk, _mm_tn(kd_, x), 0.0), tile(kd), u)
    for (ci, gi), qd_, aw_, o0_, mt_, n0_ in zip(probs, tile(qd), aw, o0, mt, n0):
        qeff_s[ci, gi] = (qd_ - aw_).astype(qeff_s.dtype)
        o0_s[ci, gi] = o0_
        mt_s[ci, gi] = mt_.astype(mt_s.dtype)
        n0_s[ci, gi] = n0_

    for ci in range(nc):
        rs = slice(ci * c, (ci + 1) * c)
        for gi in range(N_GROUPS):
            ls = slice(gi * GROUP_W, (gi + 1) * GROUP_W)
            s = s_ref[gi]
            sb = s.astype(MXU_DTYPE)
            o_s[rs, ls] = jnp.dot(qeff_s[ci, gi], sb, preferred_element_type=F32) + o0_s[ci, gi]
            s_ref[gi] = (s * blk_decay[ci * c:ci * c + 1, ls]
                         - jnp.dot(mt_s[ci, gi], sb, preferred_element_type=F32) + n0_s[ci, gi])

    o = o_s[...]
    o = o * lax.rsqrt(_head_sum(o * o, ones_h, 1) * (1.0 / hd) + RMS_EPS) * nw_ref[...]
    z = z_s[...]
    o_ref[...] = (o * (z * _sigmoid(z))).astype(o_ref.dtype)


def _gdn(h, gain, w, conv_w, gate_params, norm_w, seq, tb):
    m, d = h.shape
    nt, n_blocks = seq // tb, m // tb
    h_spec, out_spec, const = _lookahead_specs(n_blocks, tb, d)
    return pl.pallas_call(
        functools.partial(_gdn_kernel, tb=tb, nt=nt),
        grid=(n_blocks + 1,),
        in_specs=[h_spec, const((1, d)), const(w.shape),
                  const(conv_w.shape), const(gate_params.shape), const(norm_w.shape)],
        out_specs=out_spec(W_AB),
        out_shape=jax.ShapeDtypeStruct((m, W_AB), BF16),
        scratch_shapes=[pltpu.VMEM((tb, w.shape[1]), F32), pltpu.VMEM((tb, d), MXU_DTYPE),
                        pltpu.VMEM((tb, W_AB), F32), pltpu.VMEM((SUBLANES + tb, 3 * W_AB), F32)]
                       + _scan_scratch(tb),
        compiler_params=pltpu.CompilerParams(dimension_semantics=("arbitrary",),
                                             vmem_limit_bytes=VMEM_LIMIT),
        name="gdn",
    )(h, gain.reshape(1, d), w, conv_w, gate_params, norm_w)


def _rwkv_kernel(h_ref, gain_ref, w_ref, mu_ref, w0_ref, w2_ref, a0_ref, a2_ref, g2_ref, kk_ref, ka_ref,
                 rk_ref, lnw_ref, lnb_ref, o_ref,
                 p_s, u_s, xp_ref, s_ref, qeff_s, y0_s, mt_s, n0_s, y_s, *, tb, nt):
    nh, hd, c = N_HEADS_AB, HEAD_DIM, CHUNK
    nc = tb // c

    @pl.when(pl.program_id(0) == 0)
    def _():
        p_s[...] = jnp.zeros_like(p_s)

    @pl.when(_starts_sequence(nt))
    def _():
        s_ref[...] = jnp.zeros_like(s_ref)
        xp_ref[0:SUBLANES, :] = jnp.zeros((SUBLANES, xp_ref.shape[1]), F32)

    rp = p_s[...]
    fill = _project_ahead(h_ref, gain_ref, w_ref, u_s, p_s)
    xp_ref[SUBLANES:SUBLANES + tb, :] = rp
    prev = xp_ref[pl.ds(SUBLANES - 1, tb), :]
    xp_ref[0:SUBLANES, :] = rp[tb - SUBLANES:tb, :]
    xs = rp + (prev - rp) * mu_ref[...]
    r, kr, vr = xs[:, 0:W_AB], xs[:, W_AB:2 * W_AB], xs[:, 2 * W_AB:3 * W_AB]
    xwa = xs[:, 3 * W_AB:3 * W_AB + 2 * RWKV_LORA]
    xg = xs[:, 3 * W_AB + 2 * RWKV_LORA:]

    fill()
    tanh_xwa, sig_xg = jnp.tanh(xwa).astype(MXU_DTYPE), _sigmoid(xg).astype(MXU_DTYPE)
    xwa_b = xwa.astype(MXU_DTYPE)
    ones_h = _head_ones(GROUP_W, hd)
    tril = _chunk_tril(tb, c)

    names = ("gate", "k", "rh", "kkh", "bh", "kh", "bt", "kt", "decay")
    parts = {n: [] for n in names}
    for lo in range(0, W_AB, GROUP_W):
        cols = slice(lo, lo + GROUP_W)
        w_lora = jnp.dot(tanh_xwa, w2_ref[:, cols], preferred_element_type=F32)
        lw = -jnp.exp(-_softplus(-(w0_ref[:, cols] + w_lora)) - 0.5)
        a = _sigmoid(a0_ref[:, cols] + jnp.dot(xwa_b, a2_ref[:, cols], preferred_element_type=F32))
        parts["gate"].append(jnp.dot(sig_xg, g2_ref[:, cols], preferred_element_type=F32))
        fill()
        kk = kr[:, cols] * kk_ref[:, cols]
        kk = kk * lax.rsqrt(_sel_r(kk * kk, ones_h, 1) + L2_EPS)
        k = kr[:, cols] * (1.0 + (a - 1.0) * ka_ref[:, cols])
        b = kk * a
        cum = _sel_l(tril, lw, 3)
        last = _chunk_last(cum, tb, c)
        fill()
        e_neg = jnp.exp(-cum)
        e_rem = jnp.exp(last - cum)
        for n, val in (("k", k), ("rh", r[:, cols] * jnp.exp(cum)), ("kkh", kk * jnp.exp(cum - lw)),
                       ("bh", b * e_neg), ("kh", k * e_neg), ("bt", b * e_rem), ("kt", k * e_rem),
                       ("decay", jnp.exp(last))):
            parts[n].append(val)
        fill()
    whole = {n: jnp.concatenate(v, axis=1) for n, v in parts.items()}
    gate, k, chunk_decay = whole["gate"], whole["k"], whole["decay"]
    rh_all, kkh_all, bh_all, kh_all = whole["rh"], whole["kkh"], whole["bh"], whole["kh"]
    bt_all, kt_all = whole["bt"], whole["kt"]

    mk = _packed_masks()
    blk, causal, strict = mk["blk"], mk["causal"], mk["strict"]

    probs = [(ci, gi) for ci in range(nc) for gi in range(N_GROUPS)]
    tile = lambda x: [x[ci * c:(ci + 1) * c, gi * GROUP_W:(gi + 1) * GROUP_W] for ci, gi in probs]
    rh, kkh, vv, bt, kt = tile(rh_all), tile(kkh_all), tile(vr), tile(bt_all), tile(kt_all)
    lhs = _each(lambda x, y: jnp.concatenate([x, y], axis=0), kkh, rh)
    ab = _each(lambda x, y: _pmm_nt(x, _bd(y, blk)), lhs, tile(bh_all))
    ak = _each(lambda x, y: _pmm_nt(x, _bd(y, blk)), lhs, tile(kh_all))
    a_bb = _each(lambda x: jnp.where(strict, x[:c], 0.0), ab)
    a_rb = _each(lambda x: jnp.where(causal, x[c:], 0.0), ab)
    a_k = _each(lambda x: jnp.concatenate([jnp.where(strict, x[:c], 0.0), jnp.where(causal, x[c:], 0.0)],
                                          axis=0), ak)
    t = _inv_unit_lower(a_bb, mk, fill)
    fill.flush()
    z0 = _each(lambda x, y: _pmm(x, _bd(y, blk)), a_k, vv)
    wk = _each(lambda x, y: _pmm(x, _bd(y, blk)), t, kkh)
    u0 = _each(lambda x, y: -_pmm(x, _bd(y[:c], blk)), t, z0)
    ar = _each(lambda x, y: _pmm(x, _bd(y, blk)), a_rb, wk)
    y0 = _each(lambda x, y, z: _pmm(x, _bd(y, blk)) + z[c:], a_rb, u0, z0)
    mt = _each(lambda x, y: jnp.where(blk, _mm_tn(x, y), 0.0), wk, bt)
    n0 = _each(lambda u_, v_, b_, k_: jnp.where(blk, _mm_tn(jnp.concatenate([u_, v_], axis=0),
                                                            jnp.concatenate([b_, k_], axis=0)), 0.0),
               u0, vv, bt, kt)
    for (ci, gi), rh_, ar_, y0_, mt_, n0_ in zip(probs, rh, ar, y0, mt, n0):
        qeff_s[ci, gi] = (rh_ - ar_).astype(qeff_s.dtype)
        y0_s[ci, gi] = y0_
        mt_s[ci, gi] = mt_.astype(mt_s.dtype)
        n0_s[ci, gi] = n0_

    for ci in range(nc):
        rs = slice(ci * c, (ci + 1) * c)
        for gi in range(N_GROUPS):
            ls = slice(gi * GROUP_W, (gi + 1) * GROUP_W)
            s = s_ref[gi]
            sb = s.astype(MXU_DTYPE)
            y_s[rs, ls] = lax.dot_general(qeff_s[ci, gi], sb, (((1,), (1,)), ((), ())),
                                          preferred_element_type=F32) + y0_s[ci, gi]
            s_ref[gi] = (s * chunk_decay[ci * c:ci * c + 1, ls]
                         - jnp.dot(sb, mt_s[ci, gi], preferred_element_type=F32) + n0_s[ci, gi])

    y = y_s[...]
    inv_hd = 1.0 / hd
    yc = y - _head_sum(y, ones_h, 2) * inv_hd
    yn = yc * lax.rsqrt(_head_sum(yc * yc, ones_h, 1) * inv_hd + RWKV_GN_EPS) * lnw_ref[...] + lnb_ref[...]
    bonus = _head_sum(r * k * rk_ref[...], ones_h, 1) * vr
    o_ref[...] = ((yn + bonus) * gate).astype(o_ref.dtype)


def _rwkv(h, gain, w, params, seq, tb):
    m, d = h.shape
    n_in = w.shape[1]
    nt, n_blocks = seq // tb, m // tb
    h_spec, out_spec, const = _lookahead_specs(n_blocks, tb, d)
    return pl.pallas_call(
        functools.partial(_rwkv_kernel, tb=tb, nt=nt),
        grid=(n_blocks + 1,),
        in_specs=[h_spec, const((1, d)), const(w.shape)] + [const(p.shape) for p in params],
        out_specs=out_spec(W_AB),
        out_shape=jax.ShapeDtypeStruct((m, W_AB), BF16),
        scratch_shapes=[pltpu.VMEM((tb, n_in), F32), pltpu.VMEM((tb, d), MXU_DTYPE),
                        pltpu.VMEM((SUBLANES + tb, n_in), F32)] + _scan_scratch(tb),
        compiler_params=pltpu.CompilerParams(dimension_semantics=("arbitrary",),
                                             vmem_limit_bytes=VMEM_LIMIT),
        name="rwkv7",
    )(h, gain.reshape(1, d), w, *params)


def _rope_table_kernel(angle_ref, cos_ref, sin_ref):
    shape = cos_ref.shape
    pos = _iota2(shape, 0).astype(F32)
    theta = pos * angle_ref[...]
    cos_ref[...] = jnp.cos(theta)
    s = jnp.sin(theta)
    sin_ref[...] = jnp.where(_iota2(shape, 1) < RET_DK // 2, -s, s)


def _rope_tables(seq):
    angle = 1.0 / (ROPE_BASE ** jnp.linspace(0.0, 1.0, RET_DK // 2, dtype=F32))
    angle = jnp.concatenate([angle, angle]).reshape(1, RET_DK)
    return pl.pallas_call(
        _rope_table_kernel,
        out_shape=[jax.ShapeDtypeStruct((seq, RET_DK), F32)] * 2,
        name="rope_tables",
    )(angle)


def _ret_log_gamma():
    return [float(np.log1p(-np.exp2(np.float32(-5.0 - h)))) for h in range(N_HEADS_RET)]


def _ret_kernel(h_ref, gain_ref, w_ref, cos_ref, sin_ref, gnw_ref, o_ref,
                p_s, u_s, q_s, k_s, v_s, g_s, s_ref, intra_s, qdec_s, kdec_s, *, tb, nt):
    dk, dv, nh, c = RET_DK, RET_DV, N_HEADS_RET, RET_CHUNK
    wq, wv = nh * dk, nh * dv
    log_gamma = _ret_log_gamma()

    @pl.when(pl.program_id(0) == 0)
    def _():
        p_s[...] = jnp.zeros_like(p_s)
        rel = (_iota2((c, c), 0) - _iota2((c, c), 1)).astype(F32)
        idx = _iota2((c, dk), 0).astype(F32)
        for h, lg in enumerate(log_gamma):
            intra_s[h] = jnp.where(rel >= 0.0, jnp.exp(lg * jnp.maximum(rel, 0.0)), 0.0)
            qdec_s[h] = jnp.exp(lg * (idx + 1.0))
            kdec_s[h] = jnp.exp(lg * (c - 1.0 - idx))

    @pl.when(_starts_sequence(nt))
    def _():
        s_ref[...] = jnp.zeros_like(s_ref)

    fill = _project_ahead(h_ref, gain_ref, w_ref, u_s, p_s)
    cos, sin = cos_ref[...], sin_ref[...]
    for n_slab, lo in enumerate(range(0, p_s.shape[1], GROUP_W)):
        x = p_s[:, lo:lo + GROUP_W]
        if lo < 2 * wq:
            dst, off, scale = (q_s, lo, None) if lo < wq else (k_s, lo - wq, dk ** -0.5)
            for j in range(0, GROUP_W, dk):
                xh = x[:, j:j + dk]
                xr = xh * cos + pltpu.roll(xh, dk // 2, 1) * sin
                dst[:, off + j:off + j + dk] = xr if scale is None else xr * scale
        elif lo < 2 * wq + wv:
            v_s[:, lo - 2 * wq:lo - 2 * wq + GROUP_W] = x.astype(v_s.dtype)
        else:
            g_s[:, lo - 2 * wq - wv:lo - 2 * wq - wv + GROUP_W] = x * _sigmoid(x)
        if n_slab % 2 == 0:
            fill()

    heads = list(range(nh))
    for ci in range(tb // c):
        rows = slice(ci * c, (ci + 1) * c)
        q = [q_s[rows, h * dk:(h + 1) * dk] for h in heads]
        k = [k_s[rows, h * dk:(h + 1) * dk] for h in heads]
        v = [v_s[rows, h * dv:(h + 1) * dv] for h in heads]
        s = [s_ref[h] for h in heads]
        scores = _each(lambda q_, k_, h: _mm_nt(q_, k_) * intra_s[h], q, k, heads)
        fill()
        cross = _each(lambda q_, s_, h: _mm(q_ * qdec_s[h], s_), q, s, heads)
        fill()
        inner = _each(_mm, scores, v)
        fill()
        upd = _each(lambda k_, v_, h: _mm_tn(k_ * kdec_s[h], v_), k, v, heads)
        fill()
        for h, lg in enumerate(log_gamma):
            s_ref[h] = s[h] * float(np.exp(lg * c)) + upd[h]
        y = _each(jnp.add, inner, cross)
        yc = _each(lambda y_: y_ - jnp.mean(y_, axis=-1, keepdims=True), y)
        fill()
        yn = _each(lambda y_: y_ * lax.rsqrt(jnp.mean(y_ * y_, axis=-1, keepdims=True) + RET_GN_EPS), yc)
        fill()
        for h in heads:
            cols = slice(h * dv, (h + 1) * dv)
            o_ref[rows, cols] = (g_s[rows, cols] * yn[h] * gnw_ref[:, cols]).astype(o_ref.dtype)
    fill.flush()


def _retention(h, gain, w, cos, sin, gn_w, seq, tb):
    m, d = h.shape
    nt, n_blocks = seq // tb, m // tb
    nh, dk, dv, c = N_HEADS_RET, RET_DK, RET_DV, RET_CHUNK
    h_spec, out_spec, const = _lookahead_specs(n_blocks, tb, d)
    pos_blk = pl.BlockSpec((tb, dk), lambda i: (lax.rem(jnp.maximum(i - 1, 0), nt), 0))
    return pl.pallas_call(
        functools.partial(_ret_kernel, tb=tb, nt=nt),
        grid=(n_blocks + 1,),
        in_specs=[h_spec, const((1, d)), const(w.shape), pos_blk, pos_blk, const(gn_w.shape)],
        out_specs=out_spec(nh * dv),
        out_shape=jax.ShapeDtypeStruct((m, nh * dv), BF16),
        scratch_shapes=[pltpu.VMEM((tb, w.shape[1]), F32), pltpu.VMEM((tb, d), MXU_DTYPE),
                        pltpu.VMEM((tb, nh * dk), F32), pltpu.VMEM((tb, nh * dk), F32),
                        pltpu.VMEM((tb, nh * dv), MXU_DTYPE), pltpu.VMEM((tb, nh * dv), F32),
                        pltpu.VMEM((nh, dk, dv), F32), pltpu.VMEM((nh, c, c), F32),
                        pltpu.VMEM((nh, c, dk), F32), pltpu.VMEM((nh, c, dk), F32)],
        compiler_params=pltpu.CompilerParams(dimension_semantics=("arbitrary",),
                                             vmem_limit_bytes=VMEM_LIMIT),
        name="retention",
    )(h, gain.reshape(1, d), w, cos, sin, gn_w)


def _tiles(m, seq):
    tm = 512 if m % 512 == 0 else 256
    tb_ab = 256 if seq % 256 == 0 else CHUNK
    tb_ret = 256 if seq % 256 == 0 else RET_CHUNK
    return tm, tb_ab, tb_ret


def _row(v):
    return v.reshape(1, -1).astype(F32)


def _even_layer_mix(h, batch, seq, pre_gain, post_gain, w_in, conv_w, a_log, dt_bias, gdn_norm_w,
                    mu, w0, w2, a0, a2, g2, k_k, k_a, r_k, ln_w, ln_b, w_out):
    tm, tb, _ = _tiles(h.shape[0], seq)
    nh = N_HEADS_AB
    w_in = w_in.astype(MXU_DTYPE)
    gate_cols = jnp.pad(w_in[:, 4 * W_AB:4 * W_AB + 2 * nh], ((0, 0), (0, LANES - 2 * nh)))
    w_gdn = jnp.concatenate([w_in[:, :4 * W_AB], gate_cols], axis=1)
    pad_g = lambda v: jnp.pad(v.astype(F32), (nh, LANES - 2 * nh))
    gate_params = jnp.stack([pad_g(a_log), pad_g(dt_bias)])
    o_a = _gdn(h, pre_gain, w_gdn, conv_w.astype(F32), gate_params,
               _row(jnp.tile(gdn_norm_w, nh)), seq, tb)

    zeros = jnp.zeros((RWKV_LORA, W_AB), F32)
    params = [_row(mu), _row(w0), jnp.concatenate([w2, zeros]).astype(MXU_DTYPE), _row(a0),
              jnp.concatenate([zeros, a2]).astype(MXU_DTYPE), g2.astype(MXU_DTYPE),
              _row(k_k), _row(k_a), _row(r_k), _row(ln_w), _row(ln_b)]
    o_b = _rwkv(h, pre_gain, w_in[:, 4 * W_AB + 2 * nh:], params, seq, tb)

    w_out = w_out.astype(MXU_DTYPE)
    return _out_proj([o_a, o_b], [w_out[:W_AB], w_out[W_AB:]], h, post_gain, tm)


def _odd_layer_mix(h, batch, seq, pre_gain, post_gain, w_in, gn_w, w_out):
    tm, _, tb = _tiles(h.shape[0], seq)
    d = h.shape[1]
    wv = N_HEADS_RET * RET_DV
    perm = np.concatenate([np.concatenate([np.arange(0, RET_DK, 2), np.arange(1, RET_DK, 2)]) + hh * RET_DK
                           for hh in range(N_HEADS_RET)])
    w_in = w_in.astype(MXU_DTYPE)
    w_perm = jnp.concatenate([w_in[:, :d][:, perm], w_in[:, d:2 * d][:, perm], w_in[:, 2 * d:]], axis=1)
    cos, sin = _rope_tables(seq)
    y = _retention(h, pre_gain, w_perm, cos, sin, _row(gn_w), seq, tb)
    return _out_proj([y], [w_out.astype(MXU_DTYPE)], h, post_gain, tm)


def kernel(x, norm_mix_pre, norm_mix_post, norm_mlp_pre, norm_mlp_post, mlp_w_up, mlp_w_down, ab_w_in, gdn_conv_w, gdn_a_log, gdn_dt_bias, gdn_norm_w, rwkv_mu, rwkv_w0, rwkv_w2, rwkv_a0, rwkv_a2, rwkv_g2, rwkv_k_k, rwkv_k_a, rwkv_r_k, rwkv_ln_w, rwkv_ln_b, ab_w_out, ret_w_in, ret_gn_w, ret_w_out):
    batch, seq, d = x.shape
    h = x.astype(F32).reshape(batch * seq, d)
    tm = _tiles(batch * seq, seq)[0]
    for layer in range(norm_mix_pre.shape[0]):
        j = layer // 2
        if layer % 2 == 0:
            h = _even_layer_mix(h, batch, seq, norm_mix_pre[layer], norm_mix_post[layer], ab_w_in[j],
                                gdn_conv_w[j], gdn_a_log[j], gdn_dt_bias[j], gdn_norm_w[j], rwkv_mu[j],
                                rwkv_w0[j], rwkv_w2[j], rwkv_a0[j], rwkv_a2[j], rwkv_g2[j], rwkv_k_k[j],
                                rwkv_k_a[j], rwkv_r_k[j], rwkv_ln_w[j], rwkv_ln_b[j], ab_w_out[j])
        else:
            h = _odd_layer_mix(h, batch, seq, norm_mix_pre[layer], norm_mix_post[layer], ret_w_in[j],
                               ret_gn_w[j], ret_w_out[j])
        h = _mlp(h, norm_mlp_pre[layer], mlp_w_up[layer].astype(MXU_DTYPE),
                 mlp_w_down[layer].astype(MXU_DTYPE), norm_mlp_post[layer], tm, 1024)
    return h.reshape(batch, seq, d).astype(x.dtype)
```

```python
import functools

import numpy as np
import jax
import jax.numpy as jnp
from jax import lax
from jax.experimental import pallas as pl
from jax.experimental.pallas import tpu as pltpu

F32 = jnp.float32
BF16 = jnp.bfloat16
MXU_DTYPE = BF16

RMS_EPS = 1e-6
L2_EPS = 1e-6
HEAD_DIM = 64
N_HEADS_AB = 8
W_AB = N_HEADS_AB * HEAD_DIM
CHUNK = 64
SUB = 16
CONV_WIDTH = 4
RWKV_GN_EPS = 64e-5
RWKV_LORA = 64
RWKV_GATE_LORA = 128
N_HEADS_RET = 8
RET_DK = 128
RET_DV = 256
RET_CHUNK = 128
RET_GN_EPS = 1e-6
ROPE_BASE = 10000.0
LANES = 128
SUBLANES = 8
VMEM_LIMIT = 56 * 1024 * 1024


def _mm(a, b):
    return jnp.dot(a.astype(MXU_DTYPE), b.astype(MXU_DTYPE), preferred_element_type=F32)


def _mm_nt(a, b):
    return lax.dot_general(a.astype(MXU_DTYPE), b.astype(MXU_DTYPE), (((1,), (1,)), ((), ())),
                           preferred_element_type=F32)


def _mm_tn(a, b):
    return lax.dot_general(a.astype(MXU_DTYPE), b.astype(MXU_DTYPE), (((0,), (0,)), ((), ())),
                           preferred_element_type=F32)


def _split(x, n):
    parts, r = [], x
    for i in range(n):
        p = r.astype(BF16)
        parts.append(p)
        if i + 1 < n:
            r = r - p.astype(F32)
    return parts


def _sel_r(x, e, n):
    acc = None
    for p in _split(x, n):
        d = jnp.dot(p, e, preferred_element_type=F32)
        acc = d if acc is None else acc + d
    return acc


def _sel_l(e, x, n):
    acc = None
    for p in _split(x, n):
        d = jnp.dot(e, p, preferred_element_type=F32)
        acc = d if acc is None else acc + d
    return acc


def _iota2(shape, dim):
    return lax.broadcasted_iota(jnp.int32, shape, dim)


def _head_ones(width, head):
    r, c = _iota2((width, width), 0), _iota2((width, width), 1)
    return jnp.where((r // head) == (c // head), 1.0, 0.0).astype(BF16)


def _head_sum(x, ones_g, n):
    w = ones_g.shape[0]
    return jnp.concatenate([_sel_r(x[:, i:i + w], ones_g, n) for i in range(0, x.shape[1], w)], axis=1)


def _chunk_tril(tb, chunk):
    r, c = _iota2((tb, tb), 0), _iota2((tb, tb), 1)
    return jnp.where(((r // chunk) == (c // chunk)) & (c <= r), 1.0, 0.0).astype(BF16)


def _chunk_last(x, tb, chunk):
    return jnp.concatenate(
        [jnp.broadcast_to(x[(c + 1) * chunk - 1:(c + 1) * chunk, :], (chunk, x.shape[1]))
         for c in range(tb // chunk)], axis=0)


def _sigmoid(x):
    return 1.0 / (1.0 + jnp.exp(-x))


def _softplus(x):
    return jnp.maximum(x, 0.0) + jnp.log1p(jnp.exp(-jnp.abs(x)))


GROUP = 4
GROUP_W = GROUP * HEAD_DIM
N_GROUPS = N_HEADS_AB // GROUP


def _packed_masks():
    c = CHUNK
    r, l = _iota2((c, GROUP_W), 0), _iota2((c, GROUP_W), 1) % c
    rr, cc = _iota2((GROUP_W, GROUP_W), 0), _iota2((GROUP_W, GROUP_W), 1)
    return dict(causal=r >= l, strict=r > l, eye=jnp.where(r == l, 1.0, 0.0).astype(F32),
                same_sub=(r // SUB) == (l // SUB), blk=(rr // c) == (cc // c))


def _bd(x, blk):
    xb = x.astype(MXU_DTYPE)
    return jnp.where(blk, jnp.concatenate([xb] * GROUP, axis=0), jnp.zeros((), MXU_DTYPE))


def _pmm(a, ybd):
    return jnp.dot(a.astype(MXU_DTYPE), ybd, preferred_element_type=F32)


def _pmm_nt(a, ybd):
    return lax.dot_general(a.astype(MXU_DTYPE), ybd, (((1,), (1,)), ((), ())), preferred_element_type=F32)


def _each(f, *seqs):
    return [f(*args) for args in zip(*seqs)]


class _Interleave:
    def __init__(self, items):
        self._items = list(items)

    def __call__(self, n=1):
        for _ in range(min(n, len(self._items))):
            self._items.pop(0)()

    def flush(self):
        self(len(self._items))


def _project_ahead(h_ref, gain_ref, w_ref, u_s, p_s):
    u_s[...] = _rms(h_ref[...], gain_ref[...]).astype(u_s.dtype)

    def item(lo, hi):
        def run():
            p_s[:, lo:hi] = jnp.dot(u_s[...], w_ref[:, lo:hi], preferred_element_type=F32)
        return run

    n = w_ref.shape[1]
    return _Interleave([item(lo, min(lo + GROUP_W, n)) for lo in range(0, n, GROUP_W)])


def _lookahead_specs(n_blocks, tb, d):
    h_spec = pl.BlockSpec((tb, d), lambda i: (jnp.minimum(i, n_blocks - 1), 0))
    out_spec = lambda n: pl.BlockSpec((tb, n), lambda i: (jnp.maximum(i - 1, 0), 0))
    const = lambda shape: pl.BlockSpec(shape, lambda i: (0,) * len(shape), pipeline_mode=pl.Buffered(1))
    return h_spec, out_spec, const


def _starts_sequence(nt):
    i = pl.program_id(0)
    return (i == 0) | (lax.rem(i + nt - 1, nt) == 0)


def _inv_unit_lower(a, mk, fill):
    c = CHUNK

    def mm(xs, ys):
        out = _each(lambda x, y: _pmm(x, _bd(y, mk["blk"])), xs, ys)
        fill()
        return out

    def mm2(xs, zs, ys):
        out = mm(_each(lambda x, z: jnp.concatenate([x, z], axis=0), xs, zs), ys)
        return [o[:c] for o in out], [o[c:] for o in out]

    add = lambda xs, ys: _each(jnp.add, xs, ys)
    d = _each(lambda x: jnp.where(mk["same_sub"], x, 0.0), a)
    l = _each(jnp.subtract, a, d)
    d2 = mm(d, d)
    p = _each(lambda x: mk["eye"] - x, d)
    pd, d4 = mm2(p, d2, d2)
    p = add(p, pd)
    pd, d8 = mm2(p, d4, d4)
    p = add(p, pd)
    td = add(p, mm(p, d8))
    n = mm(l, td)
    n2, tn = mm2(n, td, n)
    y = _each(jnp.subtract, td, tn)
    return add(y, mm(y, n2))


def _scan_scratch(tb):
    nc = tb // CHUNK
    per = lambda rows, dt: pltpu.VMEM((nc, N_GROUPS, rows, GROUP_W), dt)
    return [pltpu.VMEM((N_GROUPS, GROUP_W, GROUP_W), F32), per(CHUNK, MXU_DTYPE), per(CHUNK, F32),
            per(GROUP_W, MXU_DTYPE), per(GROUP_W, F32), pltpu.VMEM((tb, W_AB), F32)]


def _rms(x, g):
    return x * lax.rsqrt(jnp.mean(x * x, axis=-1, keepdims=True) + RMS_EPS) * g


def _mix_mlp_kernel(*refs, n_in, ff_blk):
    a_refs, w_refs = refs[:n_in], refs[n_in:2 * n_in]
    h_ref, gmix_ref, g1_ref, wu_ref, wd_ref, g2_ref, o_ref = refs[2 * n_in:]
    mix = None
    for a_ref, w_ref in zip(a_refs, w_refs):
        d = jnp.dot(a_ref[...], w_ref[...], preferred_element_type=F32)
        mix = d if mix is None else mix + d
    x = h_ref[...] + _rms(mix, gmix_ref[...])
    u = _rms(x, g1_ref[...]).astype(MXU_DTYPE)
    acc = None
    for j in range(wu_ref.shape[1] // ff_blk):
        a = jnp.dot(u, wu_ref[:, j * ff_blk:(j + 1) * ff_blk], preferred_element_type=F32)
        a = jnp.square(jnp.maximum(a, 0.0)).astype(MXU_DTYPE)
        d = jnp.dot(a, wd_ref[j * ff_blk:(j + 1) * ff_blk, :], preferred_element_type=F32)
        acc = d if acc is None else acc + d
    o_ref[...] = x + _rms(acc, g2_ref[...])


def _mix_mlp(acts, w_outs, h, g_mix, g1, w_up, w_down, g2, tm, ff_blk):
    m, d = h.shape
    n_in = len(acts)
    rows = lambda n: pl.BlockSpec((tm, n), lambda i: (i, 0))
    resident = lambda shape: pl.BlockSpec(shape, lambda i: (0, 0), pipeline_mode=pl.Buffered(1))
    gains = [g.reshape(1, d) for g in (g_mix, g1, g2)]
    return pl.pallas_call(
        functools.partial(_mix_mlp_kernel, n_in=n_in, ff_blk=ff_blk),
        grid=(m // tm,),
        in_specs=[rows(a.shape[1]) for a in acts] + [resident(w.shape) for w in w_outs]
                 + [rows(d), resident((1, d)), resident((1, d)),
                    resident(w_up.shape), resident(w_down.shape), resident((1, d))],
        out_specs=rows(d),
        out_shape=jax.ShapeDtypeStruct((m, d), F32),
        compiler_params=pltpu.CompilerParams(dimension_semantics=("parallel",),
                                             vmem_limit_bytes=VMEM_LIMIT),
        name="mix_mlp",
    )(*acts, *w_outs, h, gains[0], gains[1], w_up, w_down, gains[2])


def _gdn_kernel(h_ref, gain_ref, w_ref, cw_ref, gp_ref, nw_ref, o_ref,
                p_s, u_s, z_s, xp_ref, s_ref, qeff_s, o0_s, mt_s, n0_s, o_s, *, tb, nt):
    nh, hd, c = N_HEADS_AB, HEAD_DIM, CHUNK
    nc = tb // c

    @pl.when(pl.program_id(0) == 0)
    def _():
        p_s[...] = jnp.zeros_like(p_s)

    @pl.when(_starts_sequence(nt))
    def _():
        s_ref[...] = jnp.zeros_like(s_ref)
        xp_ref[0:SUBLANES, :] = jnp.zeros((SUBLANES, xp_ref.shape[1]), F32)

    xp_ref[SUBLANES:SUBLANES + tb, :] = p_s[:, 0:3 * W_AB]
    z_s[...] = p_s[:, 3 * W_AB:4 * W_AB]
    gt = p_s[:, 4 * W_AB:]
    fill = _project_ahead(h_ref, gain_ref, w_ref, u_s, p_s)

    cw = cw_ref[...]
    ones_h = _head_ones(GROUP_W, hd)
    slabs = []
    for lo in range(0, 3 * W_AB, GROUP_W):
        cols = slice(lo, lo + GROUP_W)
        y = xp_ref[SUBLANES:SUBLANES + tb, cols] * cw[CONV_WIDTH - 1:CONV_WIDTH, cols]
        for j in range(CONV_WIDTH - 1):
            y = y + xp_ref[pl.ds(SUBLANES - (CONV_WIDTH - 1) + j, tb), cols] * cw[j:j + 1, cols]
        y = y * _sigmoid(y)
        if lo < 2 * W_AB:
            y = y * lax.rsqrt(_sel_r(y * y, ones_h, 1) + L2_EPS)
        slabs.append(y)
        fill()
    xp_ref[0:SUBLANES, :] = xp_ref[tb:tb + SUBLANES, :]
    per = W_AB // GROUP_W
    qn = jnp.concatenate(slabs[0:per], axis=1) * (hd ** -0.5)
    kn = jnp.concatenate(slabs[per:2 * per], axis=1)
    v = jnp.concatenate(slabs[2 * per:], axis=1)

    gp = gp_ref[...]
    beta = _sigmoid(gt)
    g = -jnp.exp(gp[0:1, :]) * _softplus(gt + gp[1:2, :])
    gcum = _sel_l(_chunk_tril(tb, c), g, 3)
    r_, c_ = _iota2((LANES, W_AB), 0), _iota2((LANES, W_AB), 1)
    pick_beta = jnp.where(r_ == c_ // hd, 1.0, 0.0).astype(BF16)
    pick_g = jnp.where(r_ == nh + c_ // hd, 1.0, 0.0).astype(BF16)
    beta_e = _sel_r(beta, pick_beta, 2)
    gc_e = _sel_r(gcum, pick_g, 3)
    glast_e = _chunk_last(gc_e, tb, c)
    gct = gcum.T
    fill()

    eg = jnp.exp(gc_e)
    kb = kn * beta_e
    kbe, vb, qd = kb * eg, v * beta_e, qn * eg
    fill()
    kd = kn * jnp.exp(glast_e - gc_e)
    blk_decay = jnp.exp(glast_e)
    fill()

    mk = _packed_masks()
    blk = mk["blk"]

    probs = [(ci, gi) for ci in range(nc) for gi in range(N_GROUPS)]
    tile = lambda x: [x[ci * c:(ci + 1) * c, gi * GROUP_W:(gi + 1) * GROUP_W] for ci, gi in probs]
    g_rows = [jnp.concatenate([gct[nh + h:nh + h + 1, ci * c:(ci + 1) * c] for h in range(nh)], axis=1)
              for ci in range(nc)]
    diff = _each(lambda gcol, p: gcol - g_rows[p[0]][:, p[1] * GROUP_W:(p[1] + 1) * GROUP_W], tile(gc_e), probs)
    dec = _each(lambda x: jnp.where(mk["causal"], jnp.exp(jnp.where(mk["causal"], x, 0.0)), 0.0), diff)
    both = _each(lambda kb_, q_, k_: _pmm_nt(jnp.concatenate([kb_, q_], axis=0), _bd(k_, blk)),
                 tile(kb), tile(qn), tile(kn))
    a = _each(lambda x, e: jnp.where(mk["strict"], x[:c] * e, 0.0), both, dec)
    attn = _each(lambda x, e: x[c:] * e, both, dec)
    t = _inv_unit_lower(a, mk, fill)
    fill.flush()
    w = _each(lambda t_, x: _pmm(t_, _bd(x, blk)), t, tile(kbe))
    u = _each(lambda t_, x: _pmm(t_, _bd(x, blk)), t, tile(vb))
    aw = _each(lambda at, x: _pmm(at, _bd(x, blk)), attn, w)
    o0 = _each(lambda at, x: _pmm(at, _bd(x, blk)), attn, u)
    mt = _each(lambda kd_, x: jnp.where(blk, _mm_tn(kd_, x), 0.0), tile(kd), w)
    n0 = _each(lambda kd_, x: jnp.where(blk, _mm_tn(kd_, x), 0.0), tile(kd), u)
    for (ci, gi), qd_, aw_, o0_, mt_, n0_ in zip(probs, tile(qd), aw, o0, mt, n0):
        qeff_s[ci, gi] = (qd_ - aw_).astype(qeff_s.dtype)
        o0_s[ci, gi] = o0_
        mt_s[ci, gi] = mt_.astype(mt_s.dtype)
        n0_s[ci, gi] = n0_

    for ci in range(nc):
        rs = slice(ci * c, (ci + 1) * c)
        for gi in range(N_GROUPS):
            ls = slice(gi * GROUP_W, (gi + 1) * GROUP_W)
            s = s_ref[gi]
            sb = s.astype(MXU_DTYPE)
            o_s[rs, ls] = jnp.dot(qeff_s[ci, gi], sb, preferred_element_type=F32) + o0_s[ci, gi]
            s_ref[gi] = (s * blk_decay[ci * c:ci * c + 1, ls]
                         - jnp.dot(mt_s[ci, gi], sb, preferred_element_type=F32) + n0_s[ci, gi])

    o = o_s[...]
    o = o * lax.rsqrt(_head_sum(o * o, ones_h, 1) * (1.0 / hd) + RMS_EPS) * nw_ref[...]
    z = z_s[...]
    o_ref[...] = (o * (z * _sigmoid(z))).astype(o_ref.dtype)


def _gdn(h, gain, w, conv_w, gate_params, norm_w, seq, tb):
    m, d = h.shape
    nt, n_blocks = seq // tb, m // tb
    h_spec, out_spec, const = _lookahead_specs(n_blocks, tb, d)
    return pl.pallas_call(
        functools.partial(_gdn_kernel, tb=tb, nt=nt),
        grid=(n_blocks + 1,),
        in_specs=[h_spec, const((1, d)), const(w.shape),
                  const(conv_w.shape), const(gate_params.shape), const(norm_w.shape)],
        out_specs=out_spec(W_AB),
        out_shape=jax.ShapeDtypeStruct((m, W_AB), BF16),
        scratch_shapes=[pltpu.VMEM((tb, w.shape[1]), F32), pltpu.VMEM((tb, d), MXU_DTYPE),
                        pltpu.VMEM((tb, W_AB), F32), pltpu.VMEM((SUBLANES + tb, 3 * W_AB), F32)]
                       + _scan_scratch(tb),
        compiler_params=pltpu.CompilerParams(dimension_semantics=("arbitrary",),
                                             vmem_limit_bytes=VMEM_LIMIT),
        name="gdn",
    )(h, gain.reshape(1, d), w, conv_w, gate_params, norm_w)


def _rwkv_kernel(h_ref, gain_ref, w_ref, mu_ref, w0_ref, w2_ref, a0_ref, a2_ref, g2_ref, kk_ref, ka_ref,
                 rk_ref, lnw_ref, lnb_ref, o_ref,
                 p_s, u_s, xp_ref, s_ref, qeff_s, y0_s, mt_s, n0_s, y_s, *, tb, nt):
    nh, hd, c = N_HEADS_AB, HEAD_DIM, CHUNK
    nc = tb // c

    @pl.when(pl.program_id(0) == 0)
    def _():
        p_s[...] = jnp.zeros_like(p_s)

    @pl.when(_starts_sequence(nt))
    def _():
        s_ref[...] = jnp.zeros_like(s_ref)
        xp_ref[0:SUBLANES, :] = jnp.zeros((SUBLANES, xp_ref.shape[1]), F32)

    rp = p_s[...]
    fill = _project_ahead(h_ref, gain_ref, w_ref, u_s, p_s)
    xp_ref[SUBLANES:SUBLANES + tb, :] = rp
    prev = xp_ref[pl.ds(SUBLANES - 1, tb), :]
    xp_ref[0:SUBLANES, :] = rp[tb - SUBLANES:tb, :]
    xs = rp + (prev - rp) * mu_ref[...]
    r, kr, vr = xs[:, 0:W_AB], xs[:, W_AB:2 * W_AB], xs[:, 2 * W_AB:3 * W_AB]
    xwa = xs[:, 3 * W_AB:3 * W_AB + 2 * RWKV_LORA]
    xg = xs[:, 3 * W_AB + 2 * RWKV_LORA:]

    fill()
    tanh_xwa, sig_xg = jnp.tanh(xwa).astype(MXU_DTYPE), _sigmoid(xg).astype(MXU_DTYPE)
    xwa_b = xwa.astype(MXU_DTYPE)
    ones_h = _head_ones(GROUP_W, hd)
    tril = _chunk_tril(tb, c)

    names = ("gate", "k", "rh", "kkh", "bh", "kh", "bt", "kt", "decay")
    parts = {n: [] for n in names}
    for lo in range(0, W_AB, GROUP_W):
        cols = slice(lo, lo + GROUP_W)
        w_lora = jnp.dot(tanh_xwa, w2_ref[:, cols], preferred_element_type=F32)
        lw = -jnp.exp(-_softplus(-(w0_ref[:, cols] + w_lora)) - 0.5)
        a = _sigmoid(a0_ref[:, cols] + jnp.dot(xwa_b, a2_ref[:, cols], preferred_element_type=F32))
        parts["gate"].append(jnp.dot(sig_xg, g2_ref[:, cols], preferred_element_type=F32))
        fill()
        kk = kr[:, cols] * kk_ref[:, cols]
        kk = kk * lax.rsqrt(_sel_r(kk * kk, ones_h, 1) + L2_EPS)
        k = kr[:, cols] * (1.0 + (a - 1.0) * ka_ref[:, cols])
        b = kk * a
        cum = _sel_l(tril, lw, 3)
        last = _chunk_last(cum, tb, c)
        fill()
        e_neg = jnp.exp(-cum)
        e_rem = jnp.exp(last - cum)
        for n, val in (("k", k), ("rh", r[:, cols] * jnp.exp(cum)), ("kkh", kk * jnp.exp(cum - lw)),
                       ("bh", b * e_neg), ("kh", k * e_neg), ("bt", b * e_rem), ("kt", k * e_rem),
                       ("decay", jnp.exp(last))):
            parts[n].append(val)
        fill()
    whole = {n: jnp.concatenate(v, axis=1) for n, v in parts.items()}
    gate, k, chunk_decay = whole["gate"], whole["k"], whole["decay"]
    rh_all, kkh_all, bh_all, kh_all = whole["rh"], whole["kkh"], whole["bh"], whole["kh"]
    bt_all, kt_all = whole["bt"], whole["kt"]

    mk = _packed_masks()
    blk, causal, strict = mk["blk"], mk["causal"], mk["strict"]

    probs = [(ci, gi) for ci in range(nc) for gi in range(N_GROUPS)]
    tile = lambda x: [x[ci * c:(ci + 1) * c, gi * GROUP_W:(gi + 1) * GROUP_W] for ci, gi in probs]
    rh, kkh, vv, bt, kt = tile(rh_all), tile(kkh_all), tile(vr), tile(bt_all), tile(kt_all)
    lhs = _each(lambda x, y: jnp.concatenate([x, y], axis=0), kkh, rh)
    ab = _each(lambda x, y: _pmm_nt(x, _bd(y, blk)), lhs, tile(bh_all))
    ak = _each(lambda x, y: _pmm_nt(x, _bd(y, blk)), lhs, tile(kh_all))
    a_bb = _each(lambda x: jnp.where(strict, x[:c], 0.0), ab)
    a_rb = _each(lambda x: jnp.where(causal, x[c:], 0.0), ab)
    a_k = _each(lambda x: jnp.concatenate([jnp.where(strict, x[:c], 0.0), jnp.where(causal, x[c:], 0.0)],
                                          axis=0), ak)
    t = _inv_unit_lower(a_bb, mk, fill)
    fill.flush()
    z0 = _each(lambda x, y: _pmm(x, _bd(y, blk)), a_k, vv)
    wk = _each(lambda x, y: _pmm(x, _bd(y, blk)), t, kkh)
    u0 = _each(lambda x, y: -_pmm(x, _bd(y[:c], blk)), t, z0)
    ar = _each(lambda x, y: _pmm(x, _bd(y, blk)), a_rb, wk)
    y0 = _each(lambda x, y, z: _pmm(x, _bd(y, blk)) + z[c:], a_rb, u0, z0)
    mt = _each(lambda x, y: jnp.where(blk, _mm_tn(x, y), 0.0), wk, bt)
    n0 = _each(lambda u_, v_, b_, k_: jnp.where(blk, _mm_tn(jnp.concatenate([u_, v_], axis=0),
                                                            jnp.concatenate([b_, k_], axis=0)), 0.0),
               u0, vv, bt, kt)
    for (ci, gi), rh_, ar_, y0_, mt_, n0_ in zip(probs, rh, ar, y0, mt, n0):
        qeff_s[ci, gi] = (rh_ - ar_).astype(qeff_s.dtype)
        y0_s[ci, gi] = y0_
        mt_s[ci, gi] = mt_.astype(mt_s.dtype)
        n0_s[ci, gi] = n0_

    for ci in range(nc):
        rs = slice(ci * c, (ci + 1) * c)
        for gi in range(N_GROUPS):
            ls = slice(gi * GROUP_W, (gi + 1) * GROUP_W)
            s = s_ref[gi]
            sb = s.astype(MXU_DTYPE)
            y_s[rs, ls] = lax.dot_general(qeff_s[ci, gi], sb, (((1,), (1,)), ((), ())),
                                          preferred_element_type=F32) + y0_s[ci, gi]
            s_ref[gi] = (s * chunk_decay[ci * c:ci * c + 1, ls]
                         - jnp.dot(sb, mt_s[ci, gi], preferred_element_type=F32) + n0_s[ci, gi])

    y = y_s[...]
    inv_hd = 1.0 / hd
    yc = y - _head_sum(y, ones_h, 2) * inv_hd
    yn = yc * lax.rsqrt(_head_sum(yc * yc, ones_h, 1) * inv_hd + RWKV_GN_EPS) * lnw_ref[...] + lnb_ref[...]
    bonus = _head_sum(r * k * rk_ref[...], ones_h, 1) * vr
    o_ref[...] = ((yn + bonus) * gate).astype(o_ref.dtype)


def _rwkv(h, gain, w, params, seq, tb):
    m, d = h.shape
    n_in = w.shape[1]
    nt, n_blocks = seq // tb, m // tb
    h_spec, out_spec, const = _lookahead_specs(n_blocks, tb, d)
    return pl.pallas_call(
        functools.partial(_rwkv_kernel, tb=tb, nt=nt),
        grid=(n_blocks + 1,),
        in_specs=[h_spec, const((1, d)), const(w.shape)] + [const(p.shape) for p in params],
        out_specs=out_spec(W_AB),
        out_shape=jax.ShapeDtypeStruct((m, W_AB), BF16),
        scratch_shapes=[pltpu.VMEM((tb, n_in), F32), pltpu.VMEM((tb, d), MXU_DTYPE),
                        pltpu.VMEM((SUBLANES + tb, n_in), F32)] + _scan_scratch(tb),
        compiler_params=pltpu.CompilerParams(dimension_semantics=("arbitrary",),
                                             vmem_limit_bytes=VMEM_LIMIT),
        name="rwkv7",
    )(h, gain.reshape(1, d), w, *params)


def _rope_table_kernel(angle_ref, cos_ref, sin_ref):
    shape = cos_ref.shape
    pos = _iota2(shape, 0).astype(F32)
    theta = pos * angle_ref[...]
    cos_ref[...] = jnp.cos(theta)
    s = jnp.sin(theta)
    sin_ref[...] = jnp.where(_iota2(shape, 1) < RET_DK // 2, -s, s)


def _rope_tables(seq):
    angle = 1.0 / (ROPE_BASE ** jnp.linspace(0.0, 1.0, RET_DK // 2, dtype=F32))
    angle = jnp.concatenate([angle, angle]).reshape(1, RET_DK)
    return pl.pallas_call(
        _rope_table_kernel,
        out_shape=[jax.ShapeDtypeStruct((seq, RET_DK), F32)] * 2,
        name="rope_tables",
    )(angle)


def _ret_log_gamma():
    return [float(np.log1p(-np.exp2(np.float32(-5.0 - h)))) for h in range(N_HEADS_RET)]


def _ret_kernel(h_ref, gain_ref, w_ref, cos_ref, sin_ref, gnw_ref, o_ref,
                p_s, u_s, q_s, k_s, v_s, g_s, s_ref, intra_s, qdec_s, kdec_s, *, tb, nt):
    dk, dv, nh, c = RET_DK, RET_DV, N_HEADS_RET, RET_CHUNK
    wq, wv = nh * dk, nh * dv
    log_gamma = _ret_log_gamma()

    @pl.when(pl.program_id(0) == 0)
    def _():
        p_s[...] = jnp.zeros_like(p_s)
        rel = (_iota2((c, c), 0) - _iota2((c, c), 1)).astype(F32)
        idx = _iota2((c, dk), 0).astype(F32)
        for h, lg in enumerate(log_gamma):
            intra_s[h] = jnp.where(rel >= 0.0, jnp.exp(lg * jnp.maximum(rel, 0.0)), 0.0)
            qdec_s[h] = jnp.exp(lg * (idx + 1.0))
            kdec_s[h] = jnp.exp(lg * (c - 1.0 - idx))

    @pl.when(_starts_sequence(nt))
    def _():
        s_ref[...] = jnp.zeros_like(s_ref)

    fill = _project_ahead(h_ref, gain_ref, w_ref, u_s, p_s)
    cos, sin = cos_ref[...], sin_ref[...]
    for n_slab, lo in enumerate(range(0, p_s.shape[1], GROUP_W)):
        x = p_s[:, lo:lo + GROUP_W]
        if lo < 2 * wq:
            dst, off, scale = (q_s, lo, None) if lo < wq else (k_s, lo - wq, dk ** -0.5)
            for j in range(0, GROUP_W, dk):
                xh = x[:, j:j + dk]
                xr = xh * cos + pltpu.roll(xh, dk // 2, 1) * sin
                dst[:, off + j:off + j + dk] = xr if scale is None else xr * scale
        elif lo < 2 * wq + wv:
            v_s[:, lo - 2 * wq:lo - 2 * wq + GROUP_W] = x.astype(v_s.dtype)
        else:
            g_s[:, lo - 2 * wq - wv:lo - 2 * wq - wv + GROUP_W] = x * _sigmoid(x)
        if n_slab % 2 == 0:
            fill()

    heads = list(range(nh))
    for ci in range(tb // c):
        rows = slice(ci * c, (ci + 1) * c)
        q = [q_s[rows, h * dk:(h + 1) * dk] for h in heads]
        k = [k_s[rows, h * dk:(h + 1) * dk] for h in heads]
        v = [v_s[rows, h * dv:(h + 1) * dv] for h in heads]
        s = [s_ref[h] for h in heads]
        scores = _each(lambda q_, k_, h: _mm_nt(q_, k_) * intra_s[h], q, k, heads)
        fill()
        cross = _each(lambda q_, s_, h: _mm(q_ * qdec_s[h], s_), q, s, heads)
        fill()
        inner = _each(_mm, scores, v)
        fill()
        upd = _each(lambda k_, v_, h: _mm_tn(k_ * kdec_s[h], v_), k, v, heads)
        fill()
        for h, lg in enumerate(log_gamma):
            s_ref[h] = s[h] * float(np.exp(lg * c)) + upd[h]
        y = _each(jnp.add, inner, cross)
        yc = _each(lambda y_: y_ - jnp.mean(y_, axis=-1, keepdims=True), y)
        fill()
        yn = _each(lambda y_: y_ * lax.rsqrt(jnp.mean(y_ * y_, axis=-1, keepdims=True) + RET_GN_EPS), yc)
        fill()
        for h in heads:
            cols = slice(h * dv, (h + 1) * dv)
            o_ref[rows, cols] = (g_s[rows, cols] * yn[h] * gnw_ref[:, cols]).astype(o_ref.dtype)
    fill.flush()


def _retention(h, gain, w, cos, sin, gn_w, seq, tb):
    m, d = h.shape
    nt, n_blocks = seq // tb, m // tb
    nh, dk, dv, c = N_HEADS_RET, RET_DK, RET_DV, RET_CHUNK
    h_spec, out_spec, const = _lookahead_specs(n_blocks, tb, d)
    pos_blk = pl.BlockSpec((tb, dk), lambda i: (lax.rem(jnp.maximum(i - 1, 0), nt), 0))
    return pl.pallas_call(
        functools.partial(_ret_kernel, tb=tb, nt=nt),
        grid=(n_blocks + 1,),
        in_specs=[h_spec, const((1, d)), const(w.shape), pos_blk, pos_blk, const(gn_w.shape)],
        out_specs=out_spec(nh * dv),
        out_shape=jax.ShapeDtypeStruct((m, nh * dv), BF16),
        scratch_shapes=[pltpu.VMEM((tb, w.shape[1]), F32), pltpu.VMEM((tb, d), MXU_DTYPE),
                        pltpu.VMEM((tb, nh * dk), F32), pltpu.VMEM((tb, nh * dk), F32),
                        pltpu.VMEM((tb, nh * dv), MXU_DTYPE), pltpu.VMEM((tb, nh * dv), F32),
                        pltpu.VMEM((nh, dk, dv), F32), pltpu.VMEM((nh, c, c), F32),
                        pltpu.VMEM((nh, c, dk), F32), pltpu.VMEM((nh, c, dk), F32)],
        compiler_params=pltpu.CompilerParams(dimension_semantics=("arbitrary",),
                                             vmem_limit_bytes=VMEM_LIMIT),
        name="retention",
    )(h, gain.reshape(1, d), w, cos, sin, gn_w)


def _tiles(m, seq):
    tm = 512 if m % 512 == 0 else 256
    tb_ab = 512 if seq % 512 == 0 else CHUNK
    tb_ret = 512 if seq % 512 == 0 else RET_CHUNK
    return tm, tb_ab, tb_ret


def _row(v):
    return v.reshape(1, -1).astype(F32)


def _even_layer_mix(h, seq, pre_gain, w_in, conv_w, a_log, dt_bias, gdn_norm_w,
                    mu, w0, w2, a0, a2, g2, k_k, k_a, r_k, ln_w, ln_b, w_out):
    _, tb, _ = _tiles(h.shape[0], seq)
    nh = N_HEADS_AB
    w_in = w_in.astype(MXU_DTYPE)
    gate_cols = jnp.pad(w_in[:, 4 * W_AB:4 * W_AB + 2 * nh], ((0, 0), (0, LANES - 2 * nh)))
    w_gdn = jnp.concatenate([w_in[:, :4 * W_AB], gate_cols], axis=1)
    pad_g = lambda v: jnp.pad(v.astype(F32), (nh, LANES - 2 * nh))
    gate_params = jnp.stack([pad_g(a_log), pad_g(dt_bias)])
    o_a = _gdn(h, pre_gain, w_gdn, conv_w.astype(F32), gate_params,
               _row(jnp.tile(gdn_norm_w, nh)), seq, tb)

    zeros = jnp.zeros((RWKV_LORA, W_AB), F32)
    params = [_row(mu), _row(w0), jnp.concatenate([w2, zeros]).astype(MXU_DTYPE), _row(a0),
              jnp.concatenate([zeros, a2]).astype(MXU_DTYPE), g2.astype(MXU_DTYPE),
              _row(k_k), _row(k_a), _row(r_k), _row(ln_w), _row(ln_b)]
    o_b = _rwkv(h, pre_gain, w_in[:, 4 * W_AB + 2 * nh:], params, seq, tb)

    w_out = w_out.astype(MXU_DTYPE)
    return [o_a, o_b], [w_out[:W_AB], w_out[W_AB:]]


def _odd_layer_mix(h, seq, pre_gain, w_in, gn_w, w_out):
    _, _, tb = _tiles(h.shape[0], seq)
    d = h.shape[1]
    wv = N_HEADS_RET * RET_DV
    perm = np.concatenate([np.concatenate([np.arange(0, RET_DK, 2), np.arange(1, RET_DK, 2)]) + hh * RET_DK
                           for hh in range(N_HEADS_RET)])
    w_in = w_in.astype(MXU_DTYPE)
    w_perm = jnp.concatenate([w_in[:, :d][:, perm], w_in[:, d:2 * d][:, perm], w_in[:, 2 * d:]], axis=1)
    cos, sin = _rope_tables(seq)
    y = _retention(h, pre_gain, w_perm, cos, sin, _row(gn_w), seq, tb)
    return [y], [w_out.astype(MXU_DTYPE)]


def kernel(x, norm_mix_pre, norm_mix_post, norm_mlp_pre, norm_mlp_post, mlp_w_up, mlp_w_down, ab_w_in, gdn_conv_w, gdn_a_log, gdn_dt_bias, gdn_norm_w, rwkv_mu, rwkv_w0, rwkv_w2, rwkv_a0, rwkv_a2, rwkv_g2, rwkv_k_k, rwkv_k_a, rwkv_r_k, rwkv_ln_w, rwkv_ln_b, ab_w_out, ret_w_in, ret_gn_w, ret_w_out):
    batch, seq, d = x.shape
    h = x.astype(F32).reshape(batch * seq, d)
    tm = _tiles(batch * seq, seq)[0]
    for layer in range(norm_mix_pre.shape[0]):
        j = layer // 2
        if layer % 2 == 0:
            acts, w_outs = _even_layer_mix(h, seq, norm_mix_pre[layer], ab_w_in[j], gdn_conv_w[j],
                                           gdn_a_log[j], gdn_dt_bias[j], gdn_norm_w[j], rwkv_mu[j], rwkv_w0[j],
                                           rwkv_w2[j], rwkv_a0[j], rwkv_a2[j], rwkv_g2[j], rwkv_k_k[j],
                                           rwkv_k_a[j], rwkv_r_k[j], rwkv_ln_w[j], rwkv_ln_b[j], ab_w_out[j])
        else:
            acts, w_outs = _odd_layer_mix(h, seq, norm_mix_pre[layer], ret_w_in[j], ret_gn_w[j], ret_w_out[j])
        h = _mix_mlp(acts, w_outs, h, norm_mix_post[layer], norm_mlp_pre[layer],
                     mlp_w_up[layer].astype(MXU_DTYPE), mlp_w_down[layer].astype(MXU_DTYPE),
                     norm_mlp_post[layer], tm, 1024)
    return h.reshape(batch, seq, d).astype(x.dtype)
```

```python
import functools

import numpy as np
import jax
import jax.numpy as jnp
from jax import lax
from jax.experimental import pallas as pl
from jax.experimental.pallas import tpu as pltpu

F32 = jnp.float32
BF16 = jnp.bfloat16
MXU_DTYPE = BF16

RMS_EPS = 1e-6
L2_EPS = 1e-6
HEAD_DIM = 64
N_HEADS_AB = 8
W_AB = N_HEADS_AB * HEAD_DIM
CHUNK = 64
SUB = 16
CONV_WIDTH = 4
RWKV_GN_EPS = 64e-5
RWKV_LORA = 64
RWKV_GATE_LORA = 128
N_HEADS_RET = 8
RET_DK = 128
RET_DV = 256
RET_CHUNK = 128
RET_GN_EPS = 1e-6
ROPE_BASE = 10000.0
LANES = 128
SUBLANES = 8
VMEM_LIMIT = 56 * 1024 * 1024


def _mm(a, b):
    return jnp.dot(a.astype(MXU_DTYPE), b.astype(MXU_DTYPE), preferred_element_type=F32)


def _mm_nt(a, b):
    return lax.dot_general(a.astype(MXU_DTYPE), b.astype(MXU_DTYPE), (((1,), (1,)), ((), ())),
                           preferred_element_type=F32)


def _mm_tn(a, b):
    return lax.dot_general(a.astype(MXU_DTYPE), b.astype(MXU_DTYPE), (((0,), (0,)), ((), ())),
                           preferred_element_type=F32)


def _split(x, n):
    parts, r = [], x
    for i in range(n):
        p = r.astype(BF16)
        parts.append(p)
        if i + 1 < n:
            r = r - p.astype(F32)
    return parts


def _sel_r(x, e, n):
    acc = None
    for p in _split(x, n):
        d = jnp.dot(p, e, preferred_element_type=F32)
        acc = d if acc is None else acc + d
    return acc


def _sel_l(e, x, n):
    acc = None
    for p in _split(x, n):
        d = jnp.dot(e, p, preferred_element_type=F32)
        acc = d if acc is None else acc + d
    return acc


def _iota2(shape, dim):
    return lax.broadcasted_iota(jnp.int32, shape, dim)


def _head_ones(width, head):
    r, c = _iota2((width, width), 0), _iota2((width, width), 1)
    return jnp.where((r // head) == (c // head), 1.0, 0.0).astype(BF16)


def _head_sum(x, ones_g, n):
    w = ones_g.shape[0]
    return jnp.concatenate([_sel_r(x[:, i:i + w], ones_g, n) for i in range(0, x.shape[1], w)], axis=1)


def _chunk_tril(tb, chunk):
    r, c = _iota2((tb, tb), 0), _iota2((tb, tb), 1)
    return jnp.where(((r // chunk) == (c // chunk)) & (c <= r), 1.0, 0.0).astype(BF16)


def _chunk_cumsum(x, chunk, n):
    rows = min(x.shape[0], GROUP_W)
    tril = _chunk_tril(rows, chunk)
    return jnp.concatenate([_sel_l(tril, x[r:r + rows], n) for r in range(0, x.shape[0], rows)], axis=0)


def _chunk_last(x, tb, chunk):
    return jnp.concatenate(
        [jnp.broadcast_to(x[(c + 1) * chunk - 1:(c + 1) * chunk, :], (chunk, x.shape[1]))
         for c in range(tb // chunk)], axis=0)


def _sigmoid(x):
    return 1.0 / (1.0 + jnp.exp(-x))


def _softplus(x):
    return jnp.maximum(x, 0.0) + jnp.log1p(jnp.exp(-jnp.abs(x)))


GROUP = 4
GROUP_W = GROUP * HEAD_DIM
N_GROUPS = N_HEADS_AB // GROUP


def _packed_masks():
    c = CHUNK
    r, l = _iota2((c, GROUP_W), 0), _iota2((c, GROUP_W), 1) % c
    rr, cc = _iota2((GROUP_W, GROUP_W), 0), _iota2((GROUP_W, GROUP_W), 1)
    return dict(causal=r >= l, strict=r > l, eye=jnp.where(r == l, 1.0, 0.0).astype(F32),
                same_sub=(r // SUB) == (l // SUB), blk=(rr // c) == (cc // c))


def _bd(x, blk):
    xb = x.astype(MXU_DTYPE)
    return jnp.where(blk, jnp.concatenate([xb] * GROUP, axis=0), jnp.zeros((), MXU_DTYPE))


def _pmm(a, ybd):
    return jnp.dot(a.astype(MXU_DTYPE), ybd, preferred_element_type=F32)


def _pmm_nt(a, ybd):
    return lax.dot_general(a.astype(MXU_DTYPE), ybd, (((1,), (1,)), ((), ())), preferred_element_type=F32)


def _each(f, *seqs):
    return [f(*args) for args in zip(*seqs)]


class _Interleave:
    def __init__(self, items):
        self._items = list(items)

    def __call__(self, n=1):
        for _ in range(min(n, len(self._items))):
            self._items.pop(0)()

    def flush(self):
        self(len(self._items))


def _project_ahead(h_ref, gain_ref, w_ref, u_s, p_s):
    u_s[...] = _rms(h_ref[...], gain_ref[...]).astype(u_s.dtype)

    def item(lo, hi):
        def run():
            p_s[:, lo:hi] = jnp.dot(u_s[...], w_ref[:, lo:hi], preferred_element_type=F32)
        return run

    n = w_ref.shape[1]
    return _Interleave([item(lo, min(lo + GROUP_W, n)) for lo in range(0, n, GROUP_W)])


def _lookahead_specs(n_blocks, tb, d):
    h_spec = pl.BlockSpec((tb, d), lambda i: (jnp.minimum(i, n_blocks - 1), 0))
    out_spec = lambda n: pl.BlockSpec((tb, n), lambda i: (jnp.maximum(i - 1, 0), 0))
    const = lambda shape: pl.BlockSpec(shape, lambda i: (0,) * len(shape), pipeline_mode=pl.Buffered(1))
    return h_spec, out_spec, const


def _starts_sequence(nt):
    i = pl.program_id(0)
    return (i == 0) | (lax.rem(i + nt - 1, nt) == 0)


def _inv_unit_lower(a, mk, fill):
    c = CHUNK

    def mm(xs, ys):
        out = _each(lambda x, y: _pmm(x, _bd(y, mk["blk"])), xs, ys)
        fill()
        return out

    def mm2(xs, zs, ys):
        out = mm(_each(lambda x, z: jnp.concatenate([x, z], axis=0), xs, zs), ys)
        return [o[:c] for o in out], [o[c:] for o in out]

    add = lambda xs, ys: _each(jnp.add, xs, ys)
    d = _each(lambda x: jnp.where(mk["same_sub"], x, 0.0), a)
    l = _each(jnp.subtract, a, d)
    d2 = mm(d, d)
    p = _each(lambda x: mk["eye"] - x, d)
    pd, d4 = mm2(p, d2, d2)
    p = add(p, pd)
    pd, d8 = mm2(p, d4, d4)
    p = add(p, pd)
    td = add(p, mm(p, d8))
    n = mm(l, td)
    n2, tn = mm2(n, td, n)
    y = _each(jnp.subtract, td, tn)
    return add(y, mm(y, n2))


def _scan_scratch(tb):
    nc = tb // CHUNK
    per = lambda rows, dt: pltpu.VMEM((nc, N_GROUPS, rows, GROUP_W), dt)
    return [pltpu.VMEM((N_GROUPS, GROUP_W, GROUP_W), F32), per(CHUNK, MXU_DTYPE), per(CHUNK, F32),
            per(GROUP_W, MXU_DTYPE), per(GROUP_W, F32), pltpu.VMEM((tb, W_AB), F32)]


def _rms(x, g):
    return x * lax.rsqrt(jnp.mean(x * x, axis=-1, keepdims=True) + RMS_EPS) * g


def _mix_mlp_kernel(*refs, n_in, ff_blk):
    a_refs, w_refs = refs[:n_in], refs[n_in:2 * n_in]
    h_ref, gmix_ref, g1_ref, wu_ref, wd_ref, g2_ref, o_ref = refs[2 * n_in:]
    mix = None
    for a_ref, w_ref in zip(a_refs, w_refs):
        d = jnp.dot(a_ref[...], w_ref[...], preferred_element_type=F32)
        mix = d if mix is None else mix + d
    x = h_ref[...] + _rms(mix, gmix_ref[...])
    u = _rms(x, g1_ref[...]).astype(MXU_DTYPE)
    acc = None
    for j in range(wu_ref.shape[1] // ff_blk):
        a = jnp.dot(u, wu_ref[:, j * ff_blk:(j + 1) * ff_blk], preferred_element_type=F32)
        a = jnp.square(jnp.maximum(a, 0.0)).astype(MXU_DTYPE)
        d = jnp.dot(a, wd_ref[j * ff_blk:(j + 1) * ff_blk, :], preferred_element_type=F32)
        acc = d if acc is None else acc + d
    o_ref[...] = x + _rms(acc, g2_ref[...])


def _mix_mlp(acts, w_outs, h, g_mix, g1, w_up, w_down, g2, tm, ff_blk):
    m, d = h.shape
    n_in = len(acts)
    rows = lambda n: pl.BlockSpec((tm, n), lambda i: (i, 0))
    resident = lambda shape: pl.BlockSpec(shape, lambda i: (0, 0), pipeline_mode=pl.Buffered(1))
    gains = [g.reshape(1, d) for g in (g_mix, g1, g2)]
    return pl.pallas_call(
        functools.partial(_mix_mlp_kernel, n_in=n_in, ff_blk=ff_blk),
        grid=(m // tm,),
        in_specs=[rows(a.shape[1]) for a in acts] + [resident(w.shape) for w in w_outs]
                 + [rows(d), resident((1, d)), resident((1, d)),
                    resident(w_up.shape), resident(w_down.shape), resident((1, d))],
        out_specs=rows(d),
        out_shape=jax.ShapeDtypeStruct((m, d), F32),
        compiler_params=pltpu.CompilerParams(dimension_semantics=("parallel",),
                                             vmem_limit_bytes=VMEM_LIMIT),
        name="mix_mlp",
    )(*acts, *w_outs, h, gains[0], gains[1], w_up, w_down, gains[2])


def _gdn_kernel(h_ref, gain_ref, w_ref, cw_ref, gp_ref, nw_ref, o_ref,
                p_s, u_s, z_s, xp_ref, s_ref, qeff_s, o0_s, mt_s, n0_s, o_s, *, tb, nt):
    nh, hd, c = N_HEADS_AB, HEAD_DIM, CHUNK
    nc = tb // c

    @pl.when(pl.program_id(0) == 0)
    def _():
        p_s[...] = jnp.zeros_like(p_s)

    @pl.when(_starts_sequence(nt))
    def _():
        s_ref[...] = jnp.zeros_like(s_ref)
        xp_ref[0:SUBLANES, :] = jnp.zeros((SUBLANES, xp_ref.shape[1]), F32)

    xp_ref[SUBLANES:SUBLANES + tb, :] = p_s[:, 0:3 * W_AB]
    z_s[...] = p_s[:, 3 * W_AB:4 * W_AB]
    gt = p_s[:, 4 * W_AB:]
    fill = _project_ahead(h_ref, gain_ref, w_ref, u_s, p_s)

    cw = cw_ref[...]
    ones_h = _head_ones(GROUP_W, hd)
    slabs = []
    for lo in range(0, 3 * W_AB, GROUP_W):
        cols = slice(lo, lo + GROUP_W)
        y = xp_ref[SUBLANES:SUBLANES + tb, cols] * cw[CONV_WIDTH - 1:CONV_WIDTH, cols]
        for j in range(CONV_WIDTH - 1):
            y = y + xp_ref[pl.ds(SUBLANES - (CONV_WIDTH - 1) + j, tb), cols] * cw[j:j + 1, cols]
        y = y * _sigmoid(y)
        if lo < 2 * W_AB:
            y = y * lax.rsqrt(_sel_r(y * y, ones_h, 1) + L2_EPS)
        slabs.append(y)
        fill()
    xp_ref[0:SUBLANES, :] = xp_ref[tb:tb + SUBLANES, :]
    per = W_AB // GROUP_W
    qn = jnp.concatenate(slabs[0:per], axis=1) * (hd ** -0.5)
    kn = jnp.concatenate(slabs[per:2 * per], axis=1)
    v = jnp.concatenate(slabs[2 * per:], axis=1)

    gp = gp_ref[...]
    beta = _sigmoid(gt)
    g = -jnp.exp(gp[0:1, :]) * _softplus(gt + gp[1:2, :])
    gcum = _chunk_cumsum(g, c, 3)
    r_, c_ = _iota2((LANES, W_AB), 0), _iota2((LANES, W_AB), 1)
    pick_beta = jnp.where(r_ == c_ // hd, 1.0, 0.0).astype(BF16)
    pick_g = jnp.where(r_ == nh + c_ // hd, 1.0, 0.0).astype(BF16)
    beta_e = _sel_r(beta, pick_beta, 2)
    gc_e = _sel_r(gcum, pick_g, 3)
    glast_e = _chunk_last(gc_e, tb, c)
    gct = gcum.T
    fill()

    eg = jnp.exp(gc_e)
    kb = kn * beta_e
    kbe, vb, qd = kb * eg, v * beta_e, qn * eg
    fill()
    kd = kn * jnp.exp(glast_e - gc_e)
    blk_decay = jnp.exp(glast_e)
    fill()

    mk = _packed_masks()
    blk = mk["blk"]

    probs = [(ci, gi) for ci in range(nc) for gi in range(N_GROUPS)]
    tile = lambda x: [x[ci * c:(ci + 1) * c, gi * GROUP_W:(gi + 1) * GROUP_W] for ci, gi in probs]
    g_rows = [jnp.concatenate([gct[nh + h:nh + h + 1, ci * c:(ci + 1) * c] for h in range(nh)], axis=1)
              for ci in range(nc)]
    diff = _each(lambda gcol, p: gcol - g_rows[p[0]][:, p[1] * GROUP_W:(p[1] + 1) * GROUP_W], tile(gc_e), probs)
    dec = _each(lambda x: jnp.where(mk["causal"], jnp.exp(jnp.where(mk["causal"], x, 0.0)), 0.0), diff)
    both = _each(lambda kb_, q_, k_: _pmm_nt(jnp.concatenate([kb_, q_], axis=0), _bd(k_, blk)),
                 tile(kb), tile(qn), tile(kn))
    a = _each(lambda x, e: jnp.where(mk["strict"], x[:c] * e, 0.0), both, dec)
    attn = _each(lambda x, e: x[c:] * e, both, dec)
    t = _inv_unit_lower(a, mk, fill)
    fill.flush()
    w = _each(lambda t_, x: _pmm(t_, _bd(x, blk)), t, tile(kbe))
    u = _each(lambda t_, x: _pmm(t_, _bd(x, blk)), t, tile(vb))
    aw = _each(lambda at, x: _pmm(at, _bd(x, blk)), attn, w)
    o0 = _each(lambda at, x: _pmm(at, _bd(x, blk)), attn, u)
    mt = _each(lambda kd_, x: jnp.where(blk, _mm_tn(kd_, x), 0.0), tile(kd), w)
    n0 = _each(lambda kd_, x: jnp.where(blk, _mm_tn(kd_, x), 0.0), tile(kd), u)
    for (ci, gi), qd_, aw_, o0_, mt_, n0_ in zip(probs, tile(qd), aw, o0, mt, n0):
        qeff_s[ci, gi] = (qd_ - aw_).astype(qeff_s.dtype)
        o0_s[ci, gi] = o0_
        mt_s[ci, gi] = mt_.astype(mt_s.dtype)
        n0_s[ci, gi] = n0_

    for ci in range(nc):
        rs = slice(ci * c, (ci + 1) * c)
        for gi in range(N_GROUPS):
            ls = slice(gi * GROUP_W, (gi + 1) * GROUP_W)
            s = s_ref[gi]
            sb = s.astype(MXU_DTYPE)
            o_s[rs, ls] = jnp.dot(qeff_s[ci, gi], sb, preferred_element_type=F32) + o0_s[ci, gi]
            s_ref[gi] = (s * blk_decay[ci * c:ci * c + 1, ls]
                         - jnp.dot(mt_s[ci, gi], sb, preferred_element_type=F32) + n0_s[ci, gi])

    o = o_s[...]
    o = o * lax.rsqrt(_head_sum(o * o, ones_h, 1) * (1.0 / hd) + RMS_EPS) * nw_ref[...]
    z = z_s[...]
    o_ref[...] = (o * (z * _sigmoid(z))).astype(o_ref.dtype)


def _gdn(h, gain, w, conv_w, gate_params, norm_w, seq, tb):
    m, d = h.shape
    nt, n_blocks = seq // tb, m // tb
    h_spec, out_spec, const = _lookahead_specs(n_blocks, tb, d)
    return pl.pallas_call(
        functools.partial(_gdn_kernel, tb=tb, nt=nt),
        grid=(n_blocks + 1,),
        in_specs=[h_spec, const((1, d)), const(w.shape),
                  const(conv_w.shape), const(gate_params.shape), const(norm_w.shape)],
        out_specs=out_spec(W_AB),
        out_shape=jax.ShapeDtypeStruct((m, W_AB), BF16),
        scratch_shapes=[pltpu.VMEM((tb, w.shape[1]), F32), pltpu.VMEM((tb, d), MXU_DTYPE),
                        pltpu.VMEM((tb, W_AB), F32), pltpu.VMEM((SUBLANES + tb, 3 * W_AB), F32)]
                       + _scan_scratch(tb),
        compiler_params=pltpu.CompilerParams(dimension_semantics=("arbitrary",),
                                             vmem_limit_bytes=VMEM_LIMIT),
        name="gdn",
    )(h, gain.reshape(1, d), w, conv_w, gate_params, norm_w)


def _rwkv_kernel(h_ref, gain_ref, w_ref, mu_ref, w0_ref, w2_ref, a0_ref, a2_ref, g2_ref, kk_ref, ka_ref,
                 rk_ref, lnw_ref, lnb_ref, o_ref,
                 p_s, u_s, xp_ref, s_ref, qeff_s, y0_s, mt_s, n0_s, y_s, *, tb, nt):
    nh, hd, c = N_HEADS_AB, HEAD_DIM, CHUNK
    nc = tb // c

    @pl.when(pl.program_id(0) == 0)
    def _():
        p_s[...] = jnp.zeros_like(p_s)

    @pl.when(_starts_sequence(nt))
    def _():
        s_ref[...] = jnp.zeros_like(s_ref)
        xp_ref[0:SUBLANES, :] = jnp.zeros((SUBLANES, xp_ref.shape[1]), F32)

    rp = p_s[...]
    fill = _project_ahead(h_ref, gain_ref, w_ref, u_s, p_s)
    xp_ref[SUBLANES:SUBLANES + tb, :] = rp
    prev = xp_ref[pl.ds(SUBLANES - 1, tb), :]
    xp_ref[0:SUBLANES, :] = rp[tb - SUBLANES:tb, :]
    xs = rp + (prev - rp) * mu_ref[...]
    r, kr, vr = xs[:, 0:W_AB], xs[:, W_AB:2 * W_AB], xs[:, 2 * W_AB:3 * W_AB]
    xwa = xs[:, 3 * W_AB:3 * W_AB + 2 * RWKV_LORA]
    xg = xs[:, 3 * W_AB + 2 * RWKV_LORA:]

    fill()
    tanh_xwa, sig_xg = jnp.tanh(xwa).astype(MXU_DTYPE), _sigmoid(xg).astype(MXU_DTYPE)
    xwa_b = xwa.astype(MXU_DTYPE)
    ones_h = _head_ones(GROUP_W, hd)

    names = ("gate", "k", "rh", "kkh", "bh", "kh", "bt", "kt", "decay")
    parts = {n: [] for n in names}
    for lo in range(0, W_AB, GROUP_W):
        cols = slice(lo, lo + GROUP_W)
        w_lora = jnp.dot(tanh_xwa, w2_ref[:, cols], preferred_element_type=F32)
        lw = -jnp.exp(-_softplus(-(w0_ref[:, cols] + w_lora)) - 0.5)
        a = _sigmoid(a0_ref[:, cols] + jnp.dot(xwa_b, a2_ref[:, cols], preferred_element_type=F32))
        parts["gate"].append(jnp.dot(sig_xg, g2_ref[:, cols], preferred_element_type=F32))
        fill()
        kk = kr[:, cols] * kk_ref[:, cols]
        kk = kk * lax.rsqrt(_sel_r(kk * kk, ones_h, 1) + L2_EPS)
        k = kr[:, cols] * (1.0 + (a - 1.0) * ka_ref[:, cols])
        b = kk * a
        cum = _chunk_cumsum(lw, c, 2)
        last = _chunk_last(cum, tb, c)
        fill()
        e_neg = jnp.exp(-cum)
        e_rem = jnp.exp(last - cum)
        for n, val in (("k", k), ("rh", r[:, cols] * jnp.exp(cum)), ("kkh", kk * jnp.exp(cum - lw)),
                       ("bh", b * e_neg), ("kh", k * e_neg), ("bt", b * e_rem), ("kt", k * e_rem),
                       ("decay", jnp.exp(last))):
            parts[n].append(val)
        fill()
    whole = {n: jnp.concatenate(v, axis=1) for n, v in parts.items()}
    gate, k, chunk_decay = whole["gate"], whole["k"], whole["decay"]
    rh_all, kkh_all, bh_all, kh_all = whole["rh"], whole["kkh"], whole["bh"], whole["kh"]
    bt_all, kt_all = whole["bt"], whole["kt"]

    mk = _packed_masks()
    blk, causal, strict = mk["blk"], mk["causal"], mk["strict"]

    probs = [(ci, gi) for ci in range(nc) for gi in range(N_GROUPS)]
    tile = lambda x: [x[ci * c:(ci + 1) * c, gi * GROUP_W:(gi + 1) * GROUP_W] for ci, gi in probs]
    rh, kkh, vv, bt, kt = tile(rh_all), tile(kkh_all), tile(vr), tile(bt_all), tile(kt_all)
    lhs = _each(lambda x, y: jnp.concatenate([x, y], axis=0), kkh, rh)
    ab = _each(lambda x, y: _pmm_nt(x, _bd(y, blk)), lhs, tile(bh_all))
    ak = _each(lambda x, y: _pmm_nt(x, _bd(y, blk)), lhs, tile(kh_all))
    a_bb = _each(lambda x: jnp.where(strict, x[:c], 0.0), ab)
    a_rb = _each(lambda x: jnp.where(causal, x[c:], 0.0), ab)
    a_k = _each(lambda x: jnp.concatenate([jnp.where(strict, x[:c], 0.0), jnp.where(causal, x[c:], 0.0)],
                                          axis=0), ak)
    t = _inv_unit_lower(a_bb, mk, fill)
    fill.flush()
    z0 = _each(lambda x, y: _pmm(x, _bd(y, blk)), a_k, vv)
    wk = _each(lambda x, y: _pmm(x, _bd(y, blk)), t, kkh)
    u0 = _each(lambda x, y: -_pmm(x, _bd(y[:c], blk)), t, z0)
    ar = _each(lambda x, y: _pmm(x, _bd(y, blk)), a_rb, wk)
    y0 = _each(lambda x, y, z: _pmm(x, _bd(y, blk)) + z[c:], a_rb, u0, z0)
    mt = _each(lambda x, y: jnp.where(blk, _mm_tn(x, y), 0.0), wk, bt)
    n0 = _each(lambda u_, v_, b_, k_: jnp.where(blk, _mm_tn(jnp.concatenate([u_, v_], axis=0),
                                                            jnp.concatenate([b_, k_], axis=0)), 0.0),
               u0, vv, bt, kt)
    for (ci, gi), rh_, ar_, y0_, mt_, n0_ in zip(probs, rh, ar, y0, mt, n0):
        qeff_s[ci, gi] = (rh_ - ar_).astype(qeff_s.dtype)
        y0_s[ci, gi] = y0_
        mt_s[ci, gi] = mt_.astype(mt_s.dtype)
        n0_s[ci, gi] = n0_

    for ci in range(nc):
        rs = slice(ci * c, (ci + 1) * c)
        for gi in range(N_GROUPS):
            ls = slice(gi * GROUP_W, (gi + 1) * GROUP_W)
            s = s_ref[gi]
            sb = s.astype(MXU_DTYPE)
            y_s[rs, ls] = lax.dot_general(qeff_s[ci, gi], sb, (((1,), (1,)), ((), ())),
                                          preferred_element_type=F32) + y0_s[ci, gi]
            s_ref[gi] = (s * chunk_decay[ci * c:ci * c + 1, ls]
                         - jnp.dot(sb, mt_s[ci, gi], preferred_element_type=F32) + n0_s[ci, gi])

    y = y_s[...]
    inv_hd = 1.0 / hd
    yc = y - _head_sum(y, ones_h, 2) * inv_hd
    yn = yc * lax.rsqrt(_head_sum(yc * yc, ones_h, 1) * inv_hd + RWKV_GN_EPS) * lnw_ref[...] + lnb_ref[...]
    bonus = _head_sum(r * k * rk_ref[...], ones_h, 1) * vr
    o_ref[...] = ((yn + bonus) * gate).astype(o_ref.dtype)


def _rwkv(h, gain, w, params, seq, tb):
    m, d = h.shape
    n_in = w.shape[1]
    nt, n_blocks = seq // tb, m // tb
    h_spec, out_spec, const = _lookahead_specs(n_blocks, tb, d)
    return pl.pallas_call(
        functools.partial(_rwkv_kernel, tb=tb, nt=nt),
        grid=(n_blocks + 1,),
        in_specs=[h_spec, const((1, d)), const(w.shape)] + [const(p.shape) for p in params],
        out_specs=out_spec(W_AB),
        out_shape=jax.ShapeDtypeStruct((m, W_AB), BF16),
        scratch_shapes=[pltpu.VMEM((tb, n_in), F32), pltpu.VMEM((tb, d), MXU_DTYPE),
                        pltpu.VMEM((SUBLANES + tb, n_in), F32)] + _scan_scratch(tb),
        compiler_params=pltpu.CompilerParams(dimension_semantics=("arbitrary",),
                                             vmem_limit_bytes=VMEM_LIMIT),
        name="rwkv7",
    )(h, gain.reshape(1, d), w, *params)


def _rope_table_kernel(angle_ref, cos_ref, sin_ref):
    shape = cos_ref.shape
    pos = _iota2(shape, 0).astype(F32)
    theta = pos * angle_ref[...]
    cos_ref[...] = jnp.cos(theta)
    s = jnp.sin(theta)
    sin_ref[...] = jnp.where(_iota2(shape, 1) < RET_DK // 2, -s, s)


def _rope_tables(seq):
    angle = 1.0 / (ROPE_BASE ** jnp.linspace(0.0, 1.0, RET_DK // 2, dtype=F32))
    angle = jnp.concatenate([angle, angle]).reshape(1, RET_DK)
    return pl.pallas_call(
        _rope_table_kernel,
        out_shape=[jax.ShapeDtypeStruct((seq, RET_DK), F32)] * 2,
        name="rope_tables",
    )(angle)


def _ret_log_gamma():
    return [float(np.log1p(-np.exp2(np.float32(-5.0 - h)))) for h in range(N_HEADS_RET)]


def _ret_kernel(h_ref, gain_ref, w_ref, cos_ref, sin_ref, gnw_ref, o_ref,
                p_s, u_s, q_s, k_s, v_s, g_s, s_ref, intra_s, qdec_s, kdec_s, *, tb, nt):
    dk, dv, nh, c = RET_DK, RET_DV, N_HEADS_RET, RET_CHUNK
    wq, wv = nh * dk, nh * dv
    log_gamma = _ret_log_gamma()

    @pl.when(pl.program_id(0) == 0)
    def _():
        p_s[...] = jnp.zeros_like(p_s)
        rel = (_iota2((c, c), 0) - _iota2((c, c), 1)).astype(F32)
        idx = _iota2((c, dk), 0).astype(F32)
        for h, lg in enumerate(log_gamma):
            intra_s[h] = jnp.where(rel >= 0.0, jnp.exp(lg * jnp.maximum(rel, 0.0)), 0.0)
            qdec_s[h] = jnp.exp(lg * (idx + 1.0))
            kdec_s[h] = jnp.exp(lg * (c - 1.0 - idx))

    @pl.when(_starts_sequence(nt))
    def _():
        s_ref[...] = jnp.zeros_like(s_ref)

    fill = _project_ahead(h_ref, gain_ref, w_ref, u_s, p_s)
    cos, sin = cos_ref[...], sin_ref[...]
    for n_slab, lo in enumerate(range(0, p_s.shape[1], GROUP_W)):
        x = p_s[:, lo:lo + GROUP_W]
        if lo < 2 * wq:
            dst, off, scale = (q_s, lo, None) if lo < wq else (k_s, lo - wq, dk ** -0.5)
            for j in range(0, GROUP_W, dk):
                xh = x[:, j:j + dk]
                xr = xh * cos + pltpu.roll(xh, dk // 2, 1) * sin
                dst[:, off + j:off + j + dk] = xr if scale is None else xr * scale
        elif lo < 2 * wq + wv:
            v_s[:, lo - 2 * wq:lo - 2 * wq + GROUP_W] = x.astype(v_s.dtype)
        else:
            g_s[:, lo - 2 * wq - wv:lo - 2 * wq - wv + GROUP_W] = x * _sigmoid(x)
        if n_slab % 2 == 0:
            fill()

    heads = list(range(nh))
    for ci in range(tb // c):
        rows = slice(ci * c, (ci + 1) * c)
        q = [q_s[rows, h * dk:(h + 1) * dk] for h in heads]
        k = [k_s[rows, h * dk:(h + 1) * dk] for h in heads]
        v = [v_s[rows, h * dv:(h + 1) * dv] for h in heads]
        s = [s_ref[h] for h in heads]
        scores = _each(lambda q_, k_, h: _mm_nt(q_, k_) * intra_s[h], q, k, heads)
        fill()
        cross = _each(lambda q_, s_, h: _mm(q_ * qdec_s[h], s_), q, s, heads)
        fill()
        inner = _each(_mm, scores, v)
        fill()
        upd = _each(lambda k_, v_, h: _mm_tn(k_ * kdec_s[h], v_), k, v, heads)
        fill()
        for h, lg in enumerate(log_gamma):
            s_ref[h] = s[h] * float(np.exp(lg * c)) + upd[h]
        y = _each(jnp.add, inner, cross)
        yc = _each(lambda y_: y_ - jnp.mean(y_, axis=-1, keepdims=True), y)
        fill()
        yn = _each(lambda y_: y_ * lax.rsqrt(jnp.mean(y_ * y_, axis=-1, keepdims=True) + RET_GN_EPS), yc)
        fill()
        for h in heads:
            cols = slice(h * dv, (h + 1) * dv)
            o_ref[rows, cols] = (g_s[rows, cols] * yn[h] * gnw_ref[:, cols]).astype(o_ref.dtype)
    fill.flush()


def _retention(h, gain, w, cos, sin, gn_w, seq, tb):
    m, d = h.shape
    nt, n_blocks = seq // tb, m // tb
    nh, dk, dv, c = N_HEADS_RET, RET_DK, RET_DV, RET_CHUNK
    h_spec, out_spec, const = _lookahead_specs(n_blocks, tb, d)
    pos_blk = pl.BlockSpec((tb, dk), lambda i: (lax.rem(jnp.maximum(i - 1, 0), nt), 0))
    return pl.pallas_call(
        functools.partial(_ret_kernel, tb=tb, nt=nt),
        grid=(n_blocks + 1,),
        in_specs=[h_spec, const((1, d)), const(w.shape), pos_blk, pos_blk, const(gn_w.shape)],
        out_specs=out_spec(nh * dv),
        out_shape=jax.ShapeDtypeStruct((m, nh * dv), BF16),
        scratch_shapes=[pltpu.VMEM((tb, w.shape[1]), F32), pltpu.VMEM((tb, d), MXU_DTYPE),
                        pltpu.VMEM((tb, nh * dk), F32), pltpu.VMEM((tb, nh * dk), F32),
                        pltpu.VMEM((tb, nh * dv), MXU_DTYPE), pltpu.VMEM((tb, nh * dv), F32),
                        pltpu.VMEM((nh, dk, dv), F32), pltpu.VMEM((nh, c, c), F32),
                        pltpu.VMEM((nh, c, dk), F32), pltpu.VMEM((nh, c, dk), F32)],
        compiler_params=pltpu.CompilerParams(dimension_semantics=("arbitrary",),
                                             vmem_limit_bytes=VMEM_LIMIT),
        name="retention",
    )(h, gain.reshape(1, d), w, cos, sin, gn_w)


def _tiles(m, seq):
    tm = 512 if m % 512 == 0 else 256
    tb_ab = 512 if seq % 512 == 0 else CHUNK
    tb_ret = 512 if seq % 512 == 0 else RET_CHUNK
    return tm, tb_ab, tb_ret


def _row(v):
    return v.reshape(1, -1).astype(F32)


def _even_layer_mix(h, seq, pre_gain, w_in, conv_w, a_log, dt_bias, gdn_norm_w,
                    mu, w0, w2, a0, a2, g2, k_k, k_a, r_k, ln_w, ln_b, w_out):
    _, tb, _ = _tiles(h.shape[0], seq)
    nh = N_HEADS_AB
    w_in = w_in.astype(MXU_DTYPE)
    gate_cols = jnp.pad(w_in[:, 4 * W_AB:4 * W_AB + 2 * nh], ((0, 0), (0, LANES - 2 * nh)))
    w_gdn = jnp.concatenate([w_in[:, :4 * W_AB], gate_cols], axis=1)
    pad_g = lambda v: jnp.pad(v.astype(F32), (nh, LANES - 2 * nh))
    gate_params = jnp.stack([pad_g(a_log), pad_g(dt_bias)])
    o_a = _gdn(h, pre_gain, w_gdn, conv_w.astype(F32), gate_params,
               _row(jnp.tile(gdn_norm_w, nh)), seq, tb)

    zeros = jnp.zeros((RWKV_LORA, W_AB), F32)
    params = [_row(mu), _row(w0), jnp.concatenate([w2, zeros]).astype(MXU_DTYPE), _row(a0),
              jnp.concatenate([zeros, a2]).astype(MXU_DTYPE), g2.astype(MXU_DTYPE),
              _row(k_k), _row(k_a), _row(r_k), _row(ln_w), _row(ln_b)]
    o_b = _rwkv(h, pre_gain, w_in[:, 4 * W_AB + 2 * nh:], params, seq, tb)

    w_out = w_out.astype(MXU_DTYPE)
    return [o_a, o_b], [w_out[:W_AB], w_out[W_AB:]]


def _odd_layer_mix(h, seq, pre_gain, w_in, gn_w, w_out):
    _, _, tb = _tiles(h.shape[0], seq)
    d = h.shape[1]
    wv = N_HEADS_RET * RET_DV
    perm = np.concatenate([np.concatenate([np.arange(0, RET_DK, 2), np.arange(1, RET_DK, 2)]) + hh * RET_DK
                           for hh in range(N_HEADS_RET)])
    w_in = w_in.astype(MXU_DTYPE)
    w_perm = jnp.concatenate([w_in[:, :d][:, perm], w_in[:, d:2 * d][:, perm], w_in[:, 2 * d:]], axis=1)
    cos, sin = _rope_tables(seq)
    y = _retention(h, pre_gain, w_perm, cos, sin, _row(gn_w), seq, tb)
    return [y], [w_out.astype(MXU_DTYPE)]


def kernel(x, norm_mix_pre, norm_mix_post, norm_mlp_pre, norm_mlp_post, mlp_w_up, mlp_w_down, ab_w_in, gdn_conv_w, gdn_a_log, gdn_dt_bias, gdn_norm_w, rwkv_mu, rwkv_w0, rwkv_w2, rwkv_a0, rwkv_a2, rwkv_g2, rwkv_k_k, rwkv_k_a, rwkv_r_k, rwkv_ln_w, rwkv_ln_b, ab_w_out, ret_w_in, ret_gn_w, ret_w_out):
    batch, seq, d = x.shape
    h = x.astype(F32).reshape(batch * seq, d)
    tm = _tiles(batch * seq, seq)[0]
    for layer in range(norm_mix_pre.shape[0]):
        j = layer // 2
        if layer % 2 == 0:
            acts, w_outs = _even_layer_mix(h, seq, norm_mix_pre[layer], ab_w_in[j], gdn_conv_w[j],
                                           gdn_a_log[j], gdn_dt_bias[j], gdn_norm_w[j], rwkv_mu[j], rwkv_w0[j],
                                           rwkv_w2[j], rwkv_a0[j], rwkv_a2[j], rwkv_g2[j], rwkv_k_k[j],
                                           rwkv_k_a[j], rwkv_r_k[j], rwkv_ln_w[j], rwkv_ln_b[j], ab_w_out[j])
        else:
            acts, w_outs = _odd_layer_mix(h, seq, norm_mix_pre[layer], ret_w_in[j], ret_gn_w[j], ret_w_out[j])
        h = _mix_mlp(acts, w_outs, h, norm_mix_post[layer], norm_mlp_pre[layer],
                     mlp_w_up[layer].astype(MXU_DTYPE), mlp_w_down[layer].astype(MXU_DTYPE),
                     norm_mlp_post[layer], tm, 1024)
    return h.reshape(batch, seq, d).astype(x.dtype)
```

```python
import functools

import numpy as np
import jax
import jax.numpy as jnp
from jax import lax
from jax.experimental import pallas as pl
from jax.experimental.pallas import tpu as pltpu

F32 = jnp.float32
BF16 = jnp.bfloat16
MXU_DTYPE = BF16

RMS_EPS = 1e-6
L2_EPS = 1e-6
HEAD_DIM = 64
N_HEADS_AB = 8
W_AB = N_HEADS_AB * HEAD_DIM
CHUNK = 64
SUB = 16
CONV_WIDTH = 4
RWKV_GN_EPS = 64e-5
RWKV_LORA = 64
RWKV_GATE_LORA = 128
N_HEADS_RET = 8
RET_DK = 128
RET_DV = 256
RET_CHUNK = 128
RET_GN_EPS = 1e-6
ROPE_BASE = 10000.0
LANES = 128
SUBLANES = 8
VMEM_LIMIT = 56 * 1024 * 1024
DENSE_ROWS = 512
MIXER_ROWS = 512
MLP_FF_BLOCK = 1024
RET_STEPS_PER_CHUNK = 6


def _mm(a, b):
    return jnp.dot(a.astype(MXU_DTYPE), b.astype(MXU_DTYPE), preferred_element_type=F32)


def _mm_nt(a, b):
    return lax.dot_general(a.astype(MXU_DTYPE), b.astype(MXU_DTYPE), (((1,), (1,)), ((), ())),
                           preferred_element_type=F32)


def _mm_tn(a, b):
    return lax.dot_general(a.astype(MXU_DTYPE), b.astype(MXU_DTYPE), (((0,), (0,)), ((), ())),
                           preferred_element_type=F32)


def _split(x, n):
    parts, r = [], x
    for i in range(n):
        p = r.astype(BF16)
        parts.append(p)
        if i + 1 < n:
            r = r - p.astype(F32)
    return parts


def _sel_r(x, e, n):
    acc = None
    for p in _split(x, n):
        d = jnp.dot(p, e, preferred_element_type=F32)
        acc = d if acc is None else acc + d
    return acc


def _sel_l(e, x, n):
    acc = None
    for p in _split(x, n):
        d = jnp.dot(e, p, preferred_element_type=F32)
        acc = d if acc is None else acc + d
    return acc


def _iota2(shape, dim):
    return lax.broadcasted_iota(jnp.int32, shape, dim)


def _head_ones(width, head):
    r, c = _iota2((width, width), 0), _iota2((width, width), 1)
    return jnp.where((r // head) == (c // head), 1.0, 0.0).astype(BF16)


def _head_sum(x, ones_g, n):
    w = ones_g.shape[0]
    return jnp.concatenate([_sel_r(x[:, i:i + w], ones_g, n) for i in range(0, x.shape[1], w)], axis=1)


def _chunk_tril(tb, chunk):
    r, c = _iota2((tb, tb), 0), _iota2((tb, tb), 1)
    return jnp.where(((r // chunk) == (c // chunk)) & (c <= r), 1.0, 0.0).astype(BF16)


def _chunk_cumsum(x, chunk, n):
    rows = min(x.shape[0], GROUP_W)
    tril = _chunk_tril(rows, chunk)
    return jnp.concatenate([_sel_l(tril, x[r:r + rows], n) for r in range(0, x.shape[0], rows)], axis=0)


def _chunk_last(x, tb, chunk):
    return jnp.concatenate(
        [jnp.broadcast_to(x[(c + 1) * chunk - 1:(c + 1) * chunk, :], (chunk, x.shape[1]))
         for c in range(tb // chunk)], axis=0)


def _sigmoid(x):
    return 1.0 / (1.0 + jnp.exp(-x))


def _softplus(x):
    return jnp.maximum(x, 0.0) + jnp.log1p(jnp.exp(-jnp.abs(x)))


GROUP = 4
GROUP_W = GROUP * HEAD_DIM
N_GROUPS = N_HEADS_AB // GROUP


def _packed_masks():
    c = CHUNK
    r, l = _iota2((c, GROUP_W), 0), _iota2((c, GROUP_W), 1) % c
    rr, cc = _iota2((GROUP_W, GROUP_W), 0), _iota2((GROUP_W, GROUP_W), 1)
    return dict(causal=r >= l, strict=r > l, eye=jnp.where(r == l, 1.0, 0.0).astype(F32),
                same_sub=(r // SUB) == (l // SUB), blk=(rr // c) == (cc // c))


def _bd(x, blk):
    xb = x.astype(MXU_DTYPE)
    return jnp.where(blk, jnp.concatenate([xb] * GROUP, axis=0), jnp.zeros((), MXU_DTYPE))


def _pmm(a, ybd):
    return jnp.dot(a.astype(MXU_DTYPE), ybd, preferred_element_type=F32)


def _pmm_nt(a, ybd):
    return lax.dot_general(a.astype(MXU_DTYPE), ybd, (((1,), (1,)), ((), ())), preferred_element_type=F32)


def _each(f, *seqs):
    return [f(*args) for args in zip(*seqs)]


class _Interleave:
    def __init__(self, items):
        self._items = list(items)

    def __call__(self, n=1):
        for _ in range(min(n, len(self._items))):
            self._items.pop(0)()

    def flush(self):
        self(len(self._items))


def _project_ahead(h_ref, gain_ref, w_ref, u_s, p_s):
    u_s[...] = _rms(h_ref[...], gain_ref[...]).astype(u_s.dtype)

    def item(lo, hi):
        def run():
            p_s[:, lo:hi] = jnp.dot(u_s[...], w_ref[:, lo:hi], preferred_element_type=F32)
        return run

    n = w_ref.shape[1]
    return _Interleave([item(lo, min(lo + GROUP_W, n)) for lo in range(0, n, GROUP_W)])


def _lookahead_specs(n_blocks, tb, d):
    h_spec = pl.BlockSpec((tb, d), lambda i: (jnp.minimum(i, n_blocks - 1), 0))
    out_spec = lambda n: pl.BlockSpec((tb, n), lambda i: (jnp.maximum(i - 1, 0), 0))
    const = lambda shape: pl.BlockSpec(shape, lambda i: (0,) * len(shape), pipeline_mode=pl.Buffered(1))
    return h_spec, out_spec, const


def _starts_sequence(nt):
    i = pl.program_id(0)
    return (i == 0) | (lax.rem(i + nt - 1, nt) == 0)


def _inv_unit_lower(a, mk, fill):
    c = CHUNK

    def mm(xs, ys):
        out = _each(lambda x, y: _pmm(x, _bd(y, mk["blk"])), xs, ys)
        fill()
        return out

    def mm2(xs, zs, ys):
        out = mm(_each(lambda x, z: jnp.concatenate([x, z], axis=0), xs, zs), ys)
        return [o[:c] for o in out], [o[c:] for o in out]

    add = lambda xs, ys: _each(jnp.add, xs, ys)
    d = _each(lambda x: jnp.where(mk["same_sub"], x, 0.0), a)
    l = _each(jnp.subtract, a, d)
    d2 = mm(d, d)
    p = _each(lambda x: mk["eye"] - x, d)
    pd, d4 = mm2(p, d2, d2)
    p = add(p, pd)
    pd, d8 = mm2(p, d4, d4)
    p = add(p, pd)
    td = add(p, mm(p, d8))
    n = mm(l, td)
    n2, tn = mm2(n, td, n)
    y = _each(jnp.subtract, td, tn)
    return add(y, mm(y, n2))


def _scan_scratch(tb):
    nc = tb // CHUNK
    per = lambda rows, dt: pltpu.VMEM((nc, N_GROUPS, rows, GROUP_W), dt)
    return [pltpu.VMEM((N_GROUPS, GROUP_W, GROUP_W), F32), per(CHUNK, MXU_DTYPE), per(CHUNK, F32),
            per(GROUP_W, MXU_DTYPE), per(GROUP_W, F32), pltpu.VMEM((tb, W_AB), F32)]


def _rms(x, g):
    return x * lax.rsqrt(jnp.mean(x * x, axis=-1, keepdims=True) + RMS_EPS) * g


def _mix_mlp_kernel(*refs, n_in, ff_blk):
    a_refs, w_refs = refs[:n_in], refs[n_in:2 * n_in]
    h_ref, gmix_ref, g1_ref, wu_ref, wd_ref, g2_ref, o_ref = refs[2 * n_in:]
    mix = None
    for a_ref, w_ref in zip(a_refs, w_refs):
        d = jnp.dot(a_ref[...], w_ref[...], preferred_element_type=F32)
        mix = d if mix is None else mix + d
    x = h_ref[...] + _rms(mix, gmix_ref[...])
    u = _rms(x, g1_ref[...]).astype(MXU_DTYPE)
    acc = None
    for j in range(wu_ref.shape[1] // ff_blk):
        a = jnp.dot(u, wu_ref[:, j * ff_blk:(j + 1) * ff_blk], preferred_element_type=F32)
        a = jnp.square(jnp.maximum(a, 0.0)).astype(MXU_DTYPE)
        d = jnp.dot(a, wd_ref[j * ff_blk:(j + 1) * ff_blk, :], preferred_element_type=F32)
        acc = d if acc is None else acc + d
    o_ref[...] = x + _rms(acc, g2_ref[...])


def _mix_mlp(acts, w_outs, h, g_mix, g1, w_up, w_down, g2, tm, ff_blk):
    m, d = h.shape
    n_in = len(acts)
    rows = lambda n: pl.BlockSpec((tm, n), lambda i: (i, 0))
    resident = lambda shape: pl.BlockSpec(shape, lambda i: (0, 0), pipeline_mode=pl.Buffered(1))
    gains = [g.reshape(1, d) for g in (g_mix, g1, g2)]
    return pl.pallas_call(
        functools.partial(_mix_mlp_kernel, n_in=n_in, ff_blk=ff_blk),
        grid=(m // tm,),
        in_specs=[rows(a.shape[1]) for a in acts] + [resident(w.shape) for w in w_outs]
                 + [rows(d), resident((1, d)), resident((1, d)),
                    resident(w_up.shape), resident(w_down.shape), resident((1, d))],
        out_specs=rows(d),
        out_shape=jax.ShapeDtypeStruct((m, d), F32),
        compiler_params=pltpu.CompilerParams(dimension_semantics=("parallel",),
                                             vmem_limit_bytes=VMEM_LIMIT),
        name="mix_mlp",
    )(*acts, *w_outs, h, gains[0], gains[1], w_up, w_down, gains[2])


def _gdn_kernel(h_ref, gain_ref, w_ref, cw_ref, gp_ref, nw_ref, o_ref,
                p_s, u_s, z_s, xp_ref, s_ref, qeff_s, o0_s, mt_s, n0_s, o_s, *, tb, nt):
    nh, hd, c = N_HEADS_AB, HEAD_DIM, CHUNK
    nc = tb // c

    @pl.when(pl.program_id(0) == 0)
    def _():
        p_s[...] = jnp.zeros_like(p_s)

    @pl.when(_starts_sequence(nt))
    def _():
        s_ref[...] = jnp.zeros_like(s_ref)
        xp_ref[0:SUBLANES, :] = jnp.zeros((SUBLANES, xp_ref.shape[1]), F32)

    xp_ref[SUBLANES:SUBLANES + tb, :] = p_s[:, 0:3 * W_AB]
    z_s[...] = p_s[:, 3 * W_AB:4 * W_AB]
    gt = p_s[:, 4 * W_AB:]
    fill = _project_ahead(h_ref, gain_ref, w_ref, u_s, p_s)

    cw = cw_ref[...]
    ones_h = _head_ones(GROUP_W, hd)
    slabs = []
    for lo in range(0, 3 * W_AB, GROUP_W):
        cols = slice(lo, lo + GROUP_W)
        y = xp_ref[SUBLANES:SUBLANES + tb, cols] * cw[CONV_WIDTH - 1:CONV_WIDTH, cols]
        for j in range(CONV_WIDTH - 1):
            y = y + xp_ref[pl.ds(SUBLANES - (CONV_WIDTH - 1) + j, tb), cols] * cw[j:j + 1, cols]
        y = y * _sigmoid(y)
        if lo < 2 * W_AB:
            y = y * lax.rsqrt(_sel_r(y * y, ones_h, 1) + L2_EPS)
        slabs.append(y)
        fill()
    xp_ref[0:SUBLANES, :] = xp_ref[tb:tb + SUBLANES, :]
    per = W_AB // GROUP_W
    qn = jnp.concatenate(slabs[0:per], axis=1) * (hd ** -0.5)
    kn = jnp.concatenate(slabs[per:2 * per], axis=1)
    v = jnp.concatenate(slabs[2 * per:], axis=1)

    gp = gp_ref[...]
    beta = _sigmoid(gt)
    g = -jnp.exp(gp[0:1, :]) * _softplus(gt + gp[1:2, :])
    gcum = _chunk_cumsum(g, c, 3)
    r_, c_ = _iota2((LANES, W_AB), 0), _iota2((LANES, W_AB), 1)
    pick_beta = jnp.where(r_ == c_ // hd, 1.0, 0.0).astype(BF16)
    pick_g = jnp.where(r_ == nh + c_ // hd, 1.0, 0.0).astype(BF16)
    beta_e = _sel_r(beta, pick_beta, 2)
    gc_e = _sel_r(gcum, pick_g, 3)
    glast_e = _chunk_last(gc_e, tb, c)
    gct = gcum.T
    fill()

    eg = jnp.exp(gc_e)
    kb = kn * beta_e
    kbe, vb, qd = kb * eg, v * beta_e, qn * eg
    fill()
    kd = kn * jnp.exp(glast_e - gc_e)
    blk_decay = jnp.exp(glast_e)
    fill()

    mk = _packed_masks()
    blk = mk["blk"]

    probs = [(ci, gi) for ci in range(nc) for gi in range(N_GROUPS)]
    tile = lambda x: [x[ci * c:(ci + 1) * c, gi * GROUP_W:(gi + 1) * GROUP_W] for ci, gi in probs]
    g_rows = [jnp.concatenate([gct[nh + h:nh + h + 1, ci * c:(ci + 1) * c] for h in range(nh)], axis=1)
              for ci in range(nc)]
    diff = _each(lambda gcol, p: gcol - g_rows[p[0]][:, p[1] * GROUP_W:(p[1] + 1) * GROUP_W], tile(gc_e), probs)
    dec = _each(lambda x: jnp.where(mk["causal"], jnp.exp(jnp.where(mk["causal"], x, 0.0)), 0.0), diff)
    both = _each(lambda kb_, q_, k_: _pmm_nt(jnp.concatenate([kb_, q_], axis=0), _bd(k_, blk)),
                 tile(kb), tile(qn), tile(kn))
    a = _each(lambda x, e: jnp.where(mk["strict"], x[:c] * e, 0.0), both, dec)
    attn = _each(lambda x, e: x[c:] * e, both, dec)
    t = _inv_unit_lower(a, mk, fill)
    fill.flush()
    w = _each(lambda t_, x: _pmm(t_, _bd(x, blk)), t, tile(kbe))
    u = _each(lambda t_, x: _pmm(t_, _bd(x, blk)), t, tile(vb))
    aw = _each(lambda at, x: _pmm(at, _bd(x, blk)), attn, w)
    o0 = _each(lambda at, x: _pmm(at, _bd(x, blk)), attn, u)
    mt = _each(lambda kd_, x: jnp.where(blk, _mm_tn(kd_, x), 0.0), tile(kd), w)
    n0 = _each(lambda kd_, x: jnp.where(blk, _mm_tn(kd_, x), 0.0), tile(kd), u)
    for (ci, gi), qd_, aw_, o0_, mt_, n0_ in zip(probs, tile(qd), aw, o0, mt, n0):
        qeff_s[ci, gi] = (qd_ - aw_).astype(qeff_s.dtype)
        o0_s[ci, gi] = o0_
        mt_s[ci, gi] = mt_.astype(mt_s.dtype)
        n0_s[ci, gi] = n0_

    for ci in range(nc):
        rs = slice(ci * c, (ci + 1) * c)
        for gi in range(N_GROUPS):
            ls = slice(gi * GROUP_W, (gi + 1) * GROUP_W)
            s = s_ref[gi]
            sb = s.astype(MXU_DTYPE)
            o_s[rs, ls] = jnp.dot(qeff_s[ci, gi], sb, preferred_element_type=F32) + o0_s[ci, gi]
            s_ref[gi] = (s * blk_decay[ci * c:ci * c + 1, ls]
                         - jnp.dot(mt_s[ci, gi], sb, preferred_element_type=F32) + n0_s[ci, gi])

    o = o_s[...]
    o = o * lax.rsqrt(_head_sum(o * o, ones_h, 1) * (1.0 / hd) + RMS_EPS) * nw_ref[...]
    z = z_s[...]
    o_ref[...] = (o * (z * _sigmoid(z))).astype(o_ref.dtype)


def _gdn(h, gain, w, conv_w, gate_params, norm_w, seq, tb):
    m, d = h.shape
    nt, n_blocks = seq // tb, m // tb
    h_spec, out_spec, const = _lookahead_specs(n_blocks, tb, d)
    return pl.pallas_call(
        functools.partial(_gdn_kernel, tb=tb, nt=nt),
        grid=(n_blocks + 1,),
        in_specs=[h_spec, const((1, d)), const(w.shape),
                  const(conv_w.shape), const(gate_params.shape), const(norm_w.shape)],
        out_specs=out_spec(W_AB),
        out_shape=jax.ShapeDtypeStruct((m, W_AB), BF16),
        scratch_shapes=[pltpu.VMEM((tb, w.shape[1]), F32), pltpu.VMEM((tb, d), MXU_DTYPE),
                        pltpu.VMEM((tb, W_AB), F32), pltpu.VMEM((SUBLANES + tb, 3 * W_AB), F32)]
                       + _scan_scratch(tb),
        compiler_params=pltpu.CompilerParams(dimension_semantics=("arbitrary",),
                                             vmem_limit_bytes=VMEM_LIMIT),
        name="gdn",
    )(h, gain.reshape(1, d), w, conv_w, gate_params, norm_w)


def _rwkv_kernel(h_ref, gain_ref, w_ref, mu_ref, w0_ref, w2_ref, a0_ref, a2_ref, g2_ref, kk_ref, ka_ref,
                 rk_ref, lnw_ref, lnb_ref, o_ref,
                 p_s, u_s, xp_ref, s_ref, qeff_s, y0_s, mt_s, n0_s, y_s, *, tb, nt):
    nh, hd, c = N_HEADS_AB, HEAD_DIM, CHUNK
    nc = tb // c

    @pl.when(pl.program_id(0) == 0)
    def _():
        p_s[...] = jnp.zeros_like(p_s)

    @pl.when(_starts_sequence(nt))
    def _():
        s_ref[...] = jnp.zeros_like(s_ref)
        xp_ref[0:SUBLANES, :] = jnp.zeros((SUBLANES, xp_ref.shape[1]), F32)

    rp = p_s[...]
    fill = _project_ahead(h_ref, gain_ref, w_ref, u_s, p_s)
    xp_ref[SUBLANES:SUBLANES + tb, :] = rp
    prev = xp_ref[pl.ds(SUBLANES - 1, tb), :]
    xp_ref[0:SUBLANES, :] = rp[tb - SUBLANES:tb, :]
    xs = rp + (prev - rp) * mu_ref[...]
    r, kr, vr = xs[:, 0:W_AB], xs[:, W_AB:2 * W_AB], xs[:, 2 * W_AB:3 * W_AB]
    xwa = xs[:, 3 * W_AB:3 * W_AB + 2 * RWKV_LORA]
    xg = xs[:, 3 * W_AB + 2 * RWKV_LORA:]

    fill()
    tanh_xwa, sig_xg = jnp.tanh(xwa).astype(MXU_DTYPE), _sigmoid(xg).astype(MXU_DTYPE)
    xwa_b = xwa.astype(MXU_DTYPE)
    ones_h = _head_ones(GROUP_W, hd)

    names = ("gate", "k", "rh", "kkh", "bh", "kh", "bt", "kt", "decay")
    parts = {n: [] for n in names}
    for lo in range(0, W_AB, GROUP_W):
        cols = slice(lo, lo + GROUP_W)
        w_lora = jnp.dot(tanh_xwa, w2_ref[:, cols], preferred_element_type=F32)
        lw = -jnp.exp(-_softplus(-(w0_ref[:, cols] + w_lora)) - 0.5)
        a = _sigmoid(a0_ref[:, cols] + jnp.dot(xwa_b, a2_ref[:, cols], preferred_element_type=F32))
        parts["gate"].append(jnp.dot(sig_xg, g2_ref[:, cols], preferred_element_type=F32))
        fill()
        kk = kr[:, cols] * kk_ref[:, cols]
        kk = kk * lax.rsqrt(_sel_r(kk * kk, ones_h, 1) + L2_EPS)
        k = kr[:, cols] * (1.0 + (a - 1.0) * ka_ref[:, cols])
        b = kk * a
        cum = _chunk_cumsum(lw, c, 2)
        last = _chunk_last(cum, tb, c)
        e_neg = jnp.exp(-cum)
        e_rem = jnp.exp(last - cum)
        for n, val in (("k", k), ("rh", r[:, cols] * jnp.exp(cum)), ("kkh", kk * jnp.exp(cum - lw)),
                       ("bh", b * e_neg), ("kh", k * e_neg), ("bt", b * e_rem), ("kt", k * e_rem),
                       ("decay", jnp.exp(last))):
            parts[n].append(val)
        fill()
    whole = {n: jnp.concatenate(v, axis=1) for n, v in parts.items()}
    gate, k, chunk_decay = whole["gate"], whole["k"], whole["decay"]
    rh_all, kkh_all, bh_all, kh_all = whole["rh"], whole["kkh"], whole["bh"], whole["kh"]
    bt_all, kt_all = whole["bt"], whole["kt"]

    mk = _packed_masks()
    blk, causal, strict = mk["blk"], mk["causal"], mk["strict"]

    probs = [(ci, gi) for ci in range(nc) for gi in range(N_GROUPS)]
    tile = lambda x: [x[ci * c:(ci + 1) * c, gi * GROUP_W:(gi + 1) * GROUP_W] for ci, gi in probs]
    rh, kkh, vv, bt, kt = tile(rh_all), tile(kkh_all), tile(vr), tile(bt_all), tile(kt_all)
    lhs = _each(lambda x, y: jnp.concatenate([x, y], axis=0), kkh, rh)
    ab = _each(lambda x, y: _pmm_nt(x, _bd(y, blk)), lhs, tile(bh_all))
    ak = _each(lambda x, y: _pmm_nt(x, _bd(y, blk)), lhs, tile(kh_all))
    a_bb = _each(lambda x: jnp.where(strict, x[:c], 0.0), ab)
    a_rb = _each(lambda x: jnp.where(causal, x[c:], 0.0), ab)
    a_k = _each(lambda x: jnp.concatenate([jnp.where(strict, x[:c], 0.0), jnp.where(causal, x[c:], 0.0)],
                                          axis=0), ak)
    t = _inv_unit_lower(a_bb, mk, fill)
    fill.flush()
    z0 = _each(lambda x, y: _pmm(x, _bd(y, blk)), a_k, vv)
    wk = _each(lambda x, y: _pmm(x, _bd(y, blk)), t, kkh)
    u0 = _each(lambda x, y: -_pmm(x, _bd(y[:c], blk)), t, z0)
    ar = _each(lambda x, y: _pmm(x, _bd(y, blk)), a_rb, wk)
    y0 = _each(lambda x, y, z: _pmm(x, _bd(y, blk)) + z[c:], a_rb, u0, z0)
    mt = _each(lambda x, y: jnp.where(blk, _mm_tn(x, y), 0.0), wk, bt)
    n0 = _each(lambda u_, v_, b_, k_: jnp.where(blk, _mm_tn(jnp.concatenate([u_, v_], axis=0),
                                                            jnp.concatenate([b_, k_], axis=0)), 0.0),
               u0, vv, bt, kt)
    for (ci, gi), rh_, ar_, y0_, mt_, n0_ in zip(probs, rh, ar, y0, mt, n0):
        qeff_s[ci, gi] = (rh_ - ar_).astype(qeff_s.dtype)
        y0_s[ci, gi] = y0_
        mt_s[ci, gi] = mt_.astype(mt_s.dtype)
        n0_s[ci, gi] = n0_

    for ci in range(nc):
        rs = slice(ci * c, (ci + 1) * c)
        for gi in range(N_GROUPS):
            ls = slice(gi * GROUP_W, (gi + 1) * GROUP_W)
            s = s_ref[gi]
            sb = s.astype(MXU_DTYPE)
            y_s[rs, ls] = lax.dot_general(qeff_s[ci, gi], sb, (((1,), (1,)), ((), ())),
                                          preferred_element_type=F32) + y0_s[ci, gi]
            s_ref[gi] = (s * chunk_decay[ci * c:ci * c + 1, ls]
                         - jnp.dot(sb, mt_s[ci, gi], preferred_element_type=F32) + n0_s[ci, gi])

    y = y_s[...]
    inv_hd = 1.0 / hd
    yc = y - _head_sum(y, ones_h, 2) * inv_hd
    yn = yc * lax.rsqrt(_head_sum(yc * yc, ones_h, 1) * inv_hd + RWKV_GN_EPS) * lnw_ref[...] + lnb_ref[...]
    bonus = _head_sum(r * k * rk_ref[...], ones_h, 1) * vr
    o_ref[...] = ((yn + bonus) * gate).astype(o_ref.dtype)


def _rwkv(h, gain, w, params, seq, tb):
    m, d = h.shape
    n_in = w.shape[1]
    nt, n_blocks = seq // tb, m // tb
    h_spec, out_spec, const = _lookahead_specs(n_blocks, tb, d)
    return pl.pallas_call(
        functools.partial(_rwkv_kernel, tb=tb, nt=nt),
        grid=(n_blocks + 1,),
        in_specs=[h_spec, const((1, d)), const(w.shape)] + [const(p.shape) for p in params],
        out_specs=out_spec(W_AB),
        out_shape=jax.ShapeDtypeStruct((m, W_AB), BF16),
        scratch_shapes=[pltpu.VMEM((tb, n_in), F32), pltpu.VMEM((tb, d), MXU_DTYPE),
                        pltpu.VMEM((SUBLANES + tb, n_in), F32)] + _scan_scratch(tb),
        compiler_params=pltpu.CompilerParams(dimension_semantics=("arbitrary",),
                                             vmem_limit_bytes=VMEM_LIMIT),
        name="rwkv7",
    )(h, gain.reshape(1, d), w, *params)


def _rope_table_kernel(angle_ref, cos_ref, sin_ref):
    shape = cos_ref.shape
    pos = _iota2(shape, 0).astype(F32)
    theta = pos * angle_ref[...]
    cos_ref[...] = jnp.cos(theta)
    s = jnp.sin(theta)
    sin_ref[...] = jnp.where(_iota2(shape, 1) < RET_DK // 2, -s, s)


def _rope_tables(seq):
    angle = 1.0 / (ROPE_BASE ** jnp.linspace(0.0, 1.0, RET_DK // 2, dtype=F32))
    angle = jnp.concatenate([angle, angle]).reshape(1, RET_DK)
    return pl.pallas_call(
        _rope_table_kernel,
        out_shape=[jax.ShapeDtypeStruct((seq, RET_DK), F32)] * 2,
        name="rope_tables",
    )(angle)


def _ret_log_gamma():
    return [float(np.log1p(-np.exp2(np.float32(-5.0 - h)))) for h in range(N_HEADS_RET)]


def _ret_kernel(h_ref, gain_ref, w_ref, cos_ref, sin_ref, gnw_ref, o_ref,
                p_s, u_s, q_s, k_s, v_s, g_s, s_ref, intra_s, qdec_s, kdec_s, *, tb, nt):
    dk, dv, nh, c = RET_DK, RET_DV, N_HEADS_RET, RET_CHUNK
    wq, wv = nh * dk, nh * dv
    log_gamma = _ret_log_gamma()

    @pl.when(pl.program_id(0) == 0)
    def _():
        p_s[...] = jnp.zeros_like(p_s)
        rel = (_iota2((c, c), 0) - _iota2((c, c), 1)).astype(F32)
        idx = _iota2((c, dk), 0).astype(F32)
        for h, lg in enumerate(log_gamma):
            intra_s[h] = jnp.where(rel >= 0.0, jnp.exp(lg * jnp.maximum(rel, 0.0)), 0.0)
            qdec_s[h] = jnp.exp(lg * (idx + 1.0))
            kdec_s[h] = jnp.exp(lg * (c - 1.0 - idx))

    @pl.when(_starts_sequence(nt))
    def _():
        s_ref[...] = jnp.zeros_like(s_ref)

    fill = _project_ahead(h_ref, gain_ref, w_ref, u_s, p_s)
    cos, sin = cos_ref[...], sin_ref[...]
    for n_slab, lo in enumerate(range(0, p_s.shape[1], GROUP_W)):
        x = p_s[:, lo:lo + GROUP_W]
        if lo < 2 * wq:
            dst, off, scale = (q_s, lo, None) if lo < wq else (k_s, lo - wq, dk ** -0.5)
            for j in range(0, GROUP_W, dk):
                xh = x[:, j:j + dk]
                xr = xh * cos + pltpu.roll(xh, dk // 2, 1) * sin
                dst[:, off + j:off + j + dk] = xr if scale is None else xr * scale
        elif lo < 2 * wq + wv:
            v_s[:, lo - 2 * wq:lo - 2 * wq + GROUP_W] = x.astype(v_s.dtype)
        else:
            g_s[:, lo - 2 * wq - wv:lo - 2 * wq - wv + GROUP_W] = x * _sigmoid(x)
        if n_slab % 3 == 0:
            fill()

    n_points = RET_STEPS_PER_CHUNK * (tb // c)
    n_left = p_s.shape[1] // GROUP_W - (p_s.shape[1] // GROUP_W + 2) // 3
    share = iter([n_left * (k + 1) // n_points - n_left * k // n_points for k in range(n_points)])
    heads = list(range(nh))
    for ci in range(tb // c):
        rows = slice(ci * c, (ci + 1) * c)
        q = [q_s[rows, h * dk:(h + 1) * dk] for h in heads]
        k = [k_s[rows, h * dk:(h + 1) * dk] for h in heads]
        v = [v_s[rows, h * dv:(h + 1) * dv] for h in heads]
        s = [s_ref[h] for h in heads]
        scores = _each(lambda q_, k_, h: _mm_nt(q_, k_) * intra_s[h], q, k, heads)
        fill(next(share))
        cross = _each(lambda q_, s_, h: _mm(q_ * qdec_s[h], s_), q, s, heads)
        fill(next(share))
        inner = _each(_mm, scores, v)
        fill(next(share))
        upd = _each(lambda k_, v_, h: _mm_tn(k_ * kdec_s[h], v_), k, v, heads)
        fill(next(share))
        for h, lg in enumerate(log_gamma):
            s_ref[h] = s[h] * float(np.exp(lg * c)) + upd[h]
        y = _each(jnp.add, inner, cross)
        yc = _each(lambda y_: y_ - jnp.mean(y_, axis=-1, keepdims=True), y)
        fill(next(share))
        yn = _each(lambda y_: y_ * lax.rsqrt(jnp.mean(y_ * y_, axis=-1, keepdims=True) + RET_GN_EPS), yc)
        fill(next(share))
        for h in heads:
            cols = slice(h * dv, (h + 1) * dv)
            o_ref[rows, cols] = (g_s[rows, cols] * yn[h] * gnw_ref[:, cols]).astype(o_ref.dtype)
    fill.flush()


def _retention(h, gain, w, cos, sin, gn_w, seq, tb):
    m, d = h.shape
    nt, n_blocks = seq // tb, m // tb
    nh, dk, dv, c = N_HEADS_RET, RET_DK, RET_DV, RET_CHUNK
    h_spec, out_spec, const = _lookahead_specs(n_blocks, tb, d)
    pos_blk = pl.BlockSpec((tb, dk), lambda i: (lax.rem(jnp.maximum(i - 1, 0), nt), 0))
    return pl.pallas_call(
        functools.partial(_ret_kernel, tb=tb, nt=nt),
        grid=(n_blocks + 1,),
        in_specs=[h_spec, const((1, d)), const(w.shape), pos_blk, pos_blk, const(gn_w.shape)],
        out_specs=out_spec(nh * dv),
        out_shape=jax.ShapeDtypeStruct((m, nh * dv), BF16),
        scratch_shapes=[pltpu.VMEM((tb, w.shape[1]), F32), pltpu.VMEM((tb, d), MXU_DTYPE),
                        pltpu.VMEM((tb, nh * dk), F32), pltpu.VMEM((tb, nh * dk), F32),
                        pltpu.VMEM((tb, nh * dv), MXU_DTYPE), pltpu.VMEM((tb, nh * dv), F32),
                        pltpu.VMEM((nh, dk, dv), F32), pltpu.VMEM((nh, c, c), F32),
                        pltpu.VMEM((nh, c, dk), F32), pltpu.VMEM((nh, c, dk), F32)],
        compiler_params=pltpu.CompilerParams(dimension_semantics=("arbitrary",),
                                             vmem_limit_bytes=VMEM_LIMIT),
        name="retention",
    )(h, gain.reshape(1, d), w, cos, sin, gn_w)


def _tiles(m, seq):
    assert m % DENSE_ROWS == 0 and seq % MIXER_ROWS == 0, (m, seq)
    return DENSE_ROWS, MIXER_ROWS, MIXER_ROWS


def _row(v):
    return v.reshape(1, -1).astype(F32)


def _even_layer_mix(h, seq, pre_gain, w_in, conv_w, a_log, dt_bias, gdn_norm_w,
                    mu, w0, w2, a0, a2, g2, k_k, k_a, r_k, ln_w, ln_b, w_out):
    _, tb, _ = _tiles(h.shape[0], seq)
    nh = N_HEADS_AB
    w_in = w_in.astype(MXU_DTYPE)
    gate_cols = jnp.pad(w_in[:, 4 * W_AB:4 * W_AB + 2 * nh], ((0, 0), (0, LANES - 2 * nh)))
    w_gdn = jnp.concatenate([w_in[:, :4 * W_AB], gate_cols], axis=1)
    pad_g = lambda v: jnp.pad(v.astype(F32), (nh, LANES - 2 * nh))
    gate_params = jnp.stack([pad_g(a_log), pad_g(dt_bias)])
    o_a = _gdn(h, pre_gain, w_gdn, conv_w.astype(F32), gate_params,
               _row(jnp.tile(gdn_norm_w, nh)), seq, tb)

    zeros = jnp.zeros((RWKV_LORA, W_AB), F32)
    params = [_row(mu), _row(w0), jnp.concatenate([w2, zeros]).astype(MXU_DTYPE), _row(a0),
              jnp.concatenate([zeros, a2]).astype(MXU_DTYPE), g2.astype(MXU_DTYPE),
              _row(k_k), _row(k_a), _row(r_k), _row(ln_w), _row(ln_b)]
    o_b = _rwkv(h, pre_gain, w_in[:, 4 * W_AB + 2 * nh:], params, seq, tb)

    w_out = w_out.astype(MXU_DTYPE)
    return [o_a, o_b], [w_out[:W_AB], w_out[W_AB:]]


def _odd_layer_mix(h, seq, pre_gain, w_in, gn_w, w_out):
    _, _, tb = _tiles(h.shape[0], seq)
    d = h.shape[1]
    wv = N_HEADS_RET * RET_DV
    perm = np.concatenate([np.concatenate([np.arange(0, RET_DK, 2), np.arange(1, RET_DK, 2)]) + hh * RET_DK
                           for hh in range(N_HEADS_RET)])
    w_in = w_in.astype(MXU_DTYPE)
    w_perm = jnp.concatenate([w_in[:, :d][:, perm], w_in[:, d:2 * d][:, perm], w_in[:, 2 * d:]], axis=1)
    cos, sin = _rope_tables(seq)
    y = _retention(h, pre_gain, w_perm, cos, sin, _row(gn_w), seq, tb)
    return [y], [w_out.astype(MXU_DTYPE)]


def kernel(x, norm_mix_pre, norm_mix_post, norm_mlp_pre, norm_mlp_post, mlp_w_up, mlp_w_down, ab_w_in, gdn_conv_w, gdn_a_log, gdn_dt_bias, gdn_norm_w, rwkv_mu, rwkv_w0, rwkv_w2, rwkv_a0, rwkv_a2, rwkv_g2, rwkv_k_k, rwkv_k_a, rwkv_r_k, rwkv_ln_w, rwkv_ln_b, ab_w_out, ret_w_in, ret_gn_w, ret_w_out):
    batch, seq, d = x.shape
    h = x.astype(F32).reshape(batch * seq, d)
    tm = _tiles(batch * seq, seq)[0]
    for layer in range(norm_mix_pre.shape[0]):
        j = layer // 2
        if layer % 2 == 0:
            acts, w_outs = _even_layer_mix(h, seq, norm_mix_pre[layer], ab_w_in[j], gdn_conv_w[j],
                                           gdn_a_log[j], gdn_dt_bias[j], gdn_norm_w[j], rwkv_mu[j], rwkv_w0[j],
                                           rwkv_w2[j], rwkv_a0[j], rwkv_a2[j], rwkv_g2[j], rwkv_k_k[j],
                                           rwkv_k_a[j], rwkv_r_k[j], rwkv_ln_w[j], rwkv_ln_b[j], ab_w_out[j])
        else:
            acts, w_outs = _odd_layer_mix(h, seq, norm_mix_pre[layer], ret_w_in[j], ret_gn_w[j], ret_w_out[j])
        h = _mix_mlp(acts, w_outs, h, norm_mix_post[layer], norm_mlp_pre[layer],
                     mlp_w_up[layer].astype(MXU_DTYPE), mlp_w_down[layer].astype(MXU_DTYPE),
                     norm_mlp_post[layer], tm, MLP_FF_BLOCK)
    return h.reshape(batch, seq, d).astype(x.dtype)
```

```python
import functools

import numpy as np
import jax
import jax.numpy as jnp
from jax import lax
from jax.experimental import pallas as pl
from jax.experimental.pallas import tpu as pltpu

F32 = jnp.float32
BF16 = jnp.bfloat16
MXU_DTYPE = BF16

RMS_EPS = 1e-6
L2_EPS = 1e-6
HEAD_DIM = 64
N_HEADS_AB = 8
W_AB = N_HEADS_AB * HEAD_DIM
CHUNK = 64
SUB = 16
CONV_WIDTH = 4
RWKV_GN_EPS = 64e-5
RWKV_LORA = 64
RWKV_GATE_LORA = 128
N_HEADS_RET = 8
RET_DK = 128
RET_DV = 256
RET_CHUNK = 128
RET_GN_EPS = 1e-6
ROPE_BASE = 10000.0
LANES = 128
SUBLANES = 8
VMEM_LIMIT = 56 * 1024 * 1024
DENSE_ROWS = 512
MIXER_ROWS = 512
MLP_FF_BLOCK = 1024
RET_STEPS_PER_CHUNK = 6


def _mm(a, b):
    return jnp.dot(a.astype(MXU_DTYPE), b.astype(MXU_DTYPE), preferred_element_type=F32)


def _mm_nt(a, b):
    return lax.dot_general(a.astype(MXU_DTYPE), b.astype(MXU_DTYPE), (((1,), (1,)), ((), ())),
                           preferred_element_type=F32)


def _mm_tn(a, b):
    return lax.dot_general(a.astype(MXU_DTYPE), b.astype(MXU_DTYPE), (((0,), (0,)), ((), ())),
                           preferred_element_type=F32)


def _split(x, n):
    parts, r = [], x
    for i in range(n):
        p = r.astype(BF16)
        parts.append(p)
        if i + 1 < n:
            r = r - p.astype(F32)
    return parts


def _sel_r(x, e, n):
    acc = None
    for p in _split(x, n):
        d = jnp.dot(p, e, preferred_element_type=F32)
        acc = d if acc is None else acc + d
    return acc


def _sel_l(e, x, n):
    acc = None
    for p in _split(x, n):
        d = jnp.dot(e, p, preferred_element_type=F32)
        acc = d if acc is None else acc + d
    return acc


def _iota2(shape, dim):
    return lax.broadcasted_iota(jnp.int32, shape, dim)


def _head_ones(width, head):
    r, c = _iota2((width, width), 0), _iota2((width, width), 1)
    return jnp.where((r // head) == (c // head), 1.0, 0.0).astype(BF16)


def _head_sum(x, ones_g, n):
    w = ones_g.shape[0]
    return jnp.concatenate([_sel_r(x[:, i:i + w], ones_g, n) for i in range(0, x.shape[1], w)], axis=1)


def _chunk_tril(tb, chunk):
    r, c = _iota2((tb, tb), 0), _iota2((tb, tb), 1)
    return jnp.where(((r // chunk) == (c // chunk)) & (c <= r), 1.0, 0.0).astype(BF16)


def _chunk_cumsum(x, chunk, n):
    rows = min(x.shape[0], GROUP_W)
    tril = _chunk_tril(rows, chunk)
    return jnp.concatenate([_sel_l(tril, x[r:r + rows], n) for r in range(0, x.shape[0], rows)], axis=0)


def _chunk_last(x, tb, chunk):
    return jnp.concatenate(
        [jnp.broadcast_to(x[(c + 1) * chunk - 1:(c + 1) * chunk, :], (chunk, x.shape[1]))
         for c in range(tb // chunk)], axis=0)


def _sigmoid(x):
    return 1.0 / (1.0 + jnp.exp(-x))


def _softplus(x):
    return jnp.maximum(x, 0.0) + jnp.log1p(jnp.exp(-jnp.abs(x)))


GROUP = 4
GROUP_W = GROUP * HEAD_DIM
N_GROUPS = N_HEADS_AB // GROUP


def _packed_masks():
    c = CHUNK
    r, l = _iota2((c, GROUP_W), 0), _iota2((c, GROUP_W), 1) % c
    rr, cc = _iota2((GROUP_W, GROUP_W), 0), _iota2((GROUP_W, GROUP_W), 1)
    return dict(causal=r >= l, strict=r > l, eye=jnp.where(r == l, 1.0, 0.0).astype(F32),
                same_sub=(r // SUB) == (l // SUB), blk=(rr // c) == (cc // c))


def _bd(x, blk):
    xb = x.astype(MXU_DTYPE)
    return jnp.where(blk, jnp.concatenate([xb] * GROUP, axis=0), jnp.zeros((), MXU_DTYPE))


def _pmm(a, ybd):
    return jnp.dot(a.astype(MXU_DTYPE), ybd, preferred_element_type=F32)


def _pmm_nt(a, ybd):
    return lax.dot_general(a.astype(MXU_DTYPE), ybd, (((1,), (1,)), ((), ())), preferred_element_type=F32)


def _each(f, *seqs):
    return [f(*args) for args in zip(*seqs)]


class _Interleave:
    def __init__(self, items):
        self._items = list(items)

    def __call__(self, n=1):
        for _ in range(min(n, len(self._items))):
            self._items.pop(0)()

    def flush(self):
        self(len(self._items))


def _project_ahead(h_ref, gain_ref, w_refs, u_s, p_s):
    u_s[...] = _rms(h_ref[...], gain_ref[...]).astype(u_s.dtype)

    def item(w_ref, lo, hi, base):
        def run():
            p_s[:, base + lo:base + hi] = jnp.dot(u_s[...], w_ref[:, lo:hi], preferred_element_type=F32)
        return run

    items, base = [], 0
    for w_ref in w_refs:
        n = w_ref.shape[1]
        items += [item(w_ref, lo, min(lo + GROUP_W, n), base) for lo in range(0, n, GROUP_W)]
        base += n
    return _Interleave(items)


def _lookahead_specs(n_blocks, tb, d):
    h_spec = pl.BlockSpec((tb, d), lambda i: (jnp.minimum(i, n_blocks - 1), 0))
    out_spec = lambda n: pl.BlockSpec((tb, n), lambda i: (jnp.maximum(i - 1, 0), 0))
    const = lambda shape: pl.BlockSpec(shape, lambda i: (0,) * len(shape), pipeline_mode=pl.Buffered(1))
    return h_spec, out_spec, const


def _starts_sequence(nt):
    i = pl.program_id(0)
    return (i == 0) | (lax.rem(i + nt - 1, nt) == 0)


def _inv_unit_lower(a, mk, fill):
    c = CHUNK

    def mm(xs, ys):
        out = _each(lambda x, y: _pmm(x, _bd(y, mk["blk"])), xs, ys)
        fill()
        return out

    def mm2(xs, zs, ys):
        out = mm(_each(lambda x, z: jnp.concatenate([x, z], axis=0), xs, zs), ys)
        return [o[:c] for o in out], [o[c:] for o in out]

    add = lambda xs, ys: _each(jnp.add, xs, ys)
    d = _each(lambda x: jnp.where(mk["same_sub"], x, 0.0), a)
    l = _each(jnp.subtract, a, d)
    d2 = mm(d, d)
    p = _each(lambda x: mk["eye"] - x, d)
    pd, d4 = mm2(p, d2, d2)
    p = add(p, pd)
    pd, d8 = mm2(p, d4, d4)
    p = add(p, pd)
    td = add(p, mm(p, d8))
    n = mm(l, td)
    n2, tn = mm2(n, td, n)
    y = _each(jnp.subtract, td, tn)
    return add(y, mm(y, n2))


def _scan_scratch(tb):
    nc = tb // CHUNK
    per = lambda rows, dt: pltpu.VMEM((nc, N_GROUPS, rows, GROUP_W), dt)
    return [pltpu.VMEM((N_GROUPS, GROUP_W, GROUP_W), F32), per(CHUNK, MXU_DTYPE), per(CHUNK, F32),
            per(GROUP_W, MXU_DTYPE), per(GROUP_W, F32), pltpu.VMEM((tb, W_AB), F32)]


def _rms(x, g):
    return x * lax.rsqrt(jnp.mean(x * x, axis=-1, keepdims=True) + RMS_EPS) * g


def _mix_mlp_kernel(*refs, n_in, ff_blk):
    a_refs, w_refs = refs[:n_in], refs[n_in:2 * n_in]
    h_ref, gmix_ref, g1_ref, wu_ref, wd_ref, g2_ref, o_ref = refs[2 * n_in:]
    mix = None
    for a_ref, w_ref in zip(a_refs, w_refs):
        d = jnp.dot(a_ref[...], w_ref[...], preferred_element_type=F32)
        mix = d if mix is None else mix + d
    x = h_ref[...] + _rms(mix, gmix_ref[...])
    u = _rms(x, g1_ref[...]).astype(MXU_DTYPE)
    acc = None
    for j in range(wu_ref.shape[1] // ff_blk):
        a = jnp.dot(u, wu_ref[:, j * ff_blk:(j + 1) * ff_blk], preferred_element_type=F32)
        a = jnp.square(jnp.maximum(a, 0.0)).astype(MXU_DTYPE)
        d = jnp.dot(a, wd_ref[j * ff_blk:(j + 1) * ff_blk, :], preferred_element_type=F32)
        acc = d if acc is None else acc + d
    o_ref[...] = x + _rms(acc, g2_ref[...])


def _mix_mlp(acts, w_outs, h, g_mix, g1, w_up, w_down, g2, tm, ff_blk):
    m, d = h.shape
    n_in = len(acts)
    rows = lambda n: pl.BlockSpec((tm, n), lambda i: (i, 0))
    resident = lambda shape: pl.BlockSpec(shape, lambda i: (0, 0), pipeline_mode=pl.Buffered(1))
    gains = [g.reshape(1, d) for g in (g_mix, g1, g2)]
    return pl.pallas_call(
        functools.partial(_mix_mlp_kernel, n_in=n_in, ff_blk=ff_blk),
        grid=(m // tm,),
        in_specs=[rows(a.shape[1]) for a in acts] + [resident(w.shape) for w in w_outs]
                 + [rows(d), resident((1, d)), resident((1, d)),
                    resident(w_up.shape), resident(w_down.shape), resident((1, d))],
        out_specs=rows(d),
        out_shape=jax.ShapeDtypeStruct((m, d), F32),
        compiler_params=pltpu.CompilerParams(dimension_semantics=("parallel",),
                                             vmem_limit_bytes=VMEM_LIMIT),
        name="mix_mlp",
    )(*acts, *w_outs, h, gains[0], gains[1], w_up, w_down, gains[2])


def _gdn_kernel(h_ref, gain_ref, w_ref, wg_ref, cw_ref, gp_ref, nw_ref, o_ref,
                p_s, u_s, z_s, xp_ref, s_ref, qeff_s, o0_s, mt_s, n0_s, o_s, *, tb, nt):
    nh, hd, c = N_HEADS_AB, HEAD_DIM, CHUNK
    nc = tb // c

    @pl.when(pl.program_id(0) == 0)
    def _():
        p_s[...] = jnp.zeros_like(p_s)

    @pl.when(_starts_sequence(nt))
    def _():
        s_ref[...] = jnp.zeros_like(s_ref)
        xp_ref[0:SUBLANES, :] = jnp.zeros((SUBLANES, xp_ref.shape[1]), F32)

    xp_ref[SUBLANES:SUBLANES + tb, :] = p_s[:, 0:3 * W_AB]
    z_s[...] = p_s[:, 3 * W_AB:4 * W_AB]
    gt = p_s[:, 4 * W_AB:]
    fill = _project_ahead(h_ref, gain_ref, [w_ref, wg_ref], u_s, p_s)

    cw = cw_ref[...]
    ones_h = _head_ones(GROUP_W, hd)
    slabs = []
    for lo in range(0, 3 * W_AB, GROUP_W):
        cols = slice(lo, lo + GROUP_W)
        y = xp_ref[SUBLANES:SUBLANES + tb, cols] * cw[CONV_WIDTH - 1:CONV_WIDTH, cols]
        for j in range(CONV_WIDTH - 1):
            y = y + xp_ref[pl.ds(SUBLANES - (CONV_WIDTH - 1) + j, tb), cols] * cw[j:j + 1, cols]
        y = y * _sigmoid(y)
        if lo < 2 * W_AB:
            y = y * lax.rsqrt(_sel_r(y * y, ones_h, 1) + L2_EPS)
        slabs.append(y)
        fill()
    xp_ref[0:SUBLANES, :] = xp_ref[tb:tb + SUBLANES, :]
    per = W_AB // GROUP_W
    qn = jnp.concatenate(slabs[0:per], axis=1) * (hd ** -0.5)
    kn = jnp.concatenate(slabs[per:2 * per], axis=1)
    v = jnp.concatenate(slabs[2 * per:], axis=1)

    gp = gp_ref[...]
    beta = _sigmoid(gt)
    g = -jnp.exp(gp[0:1, :]) * _softplus(gt + gp[1:2, :])
    gcum = _chunk_cumsum(g, c, 3)
    r_, c_ = _iota2((LANES, W_AB), 0), _iota2((LANES, W_AB), 1)
    pick_beta = jnp.where(r_ == c_ // hd, 1.0, 0.0).astype(BF16)
    pick_g = jnp.where(r_ == nh + c_ // hd, 1.0, 0.0).astype(BF16)
    beta_e = _sel_r(beta, pick_beta, 2)
    gc_e = _sel_r(gcum, pick_g, 3)
    glast_e = _chunk_last(gc_e, tb, c)
    gct = gcum.T
    fill()

    eg = jnp.exp(gc_e)
    kb = kn * beta_e
    kbe, vb, qd = kb * eg, v * beta_e, qn * eg
    fill()
    kd = kn * jnp.exp(glast_e - gc_e)
    blk_decay = jnp.exp(glast_e)
    fill()

    mk = _packed_masks()
    blk = mk["blk"]

    probs = [(ci, gi) for ci in range(nc) for gi in range(N_GROUPS)]
    tile = lambda x: [x[ci * c:(ci + 1) * c, gi * GROUP_W:(gi + 1) * GROUP_W] for ci, gi in probs]
    g_rows = [jnp.concatenate([gct[nh + h:nh + h + 1, ci * c:(ci + 1) * c] for h in range(nh)], axis=1)
              for ci in range(nc)]
    diff = _each(lambda gcol, p: gcol - g_rows[p[0]][:, p[1] * GROUP_W:(p[1] + 1) * GROUP_W], tile(gc_e), probs)
    dec = _each(lambda x: jnp.where(mk["causal"], jnp.exp(jnp.where(mk["causal"], x, 0.0)), 0.0), diff)
    both = _each(lambda kb_, q_, k_: _pmm_nt(jnp.concatenate([kb_, q_], axis=0), _bd(k_, blk)),
                 tile(kb), tile(qn), tile(kn))
    a = _each(lambda x, e: jnp.where(mk["strict"], x[:c] * e, 0.0), both, dec)
    attn = _each(lambda x, e: x[c:] * e, both, dec)
    t = _inv_unit_lower(a, mk, fill)
    fill.flush()
    w = _each(lambda t_, x: _pmm(t_, _bd(x, blk)), t, tile(kbe))
    u = _each(lambda t_, x: _pmm(t_, _bd(x, blk)), t, tile(vb))
    aw = _each(lambda at, x: _pmm(at, _bd(x, blk)), attn, w)
    o0 = _each(lambda at, x: _pmm(at, _bd(x, blk)), attn, u)
    mt = _each(lambda kd_, x: jnp.where(blk, _mm_tn(kd_, x), 0.0), tile(kd), w)
    n0 = _each(lambda kd_, x: jnp.where(blk, _mm_tn(kd_, x), 0.0), tile(kd), u)
    for (ci, gi), qd_, aw_, o0_, mt_, n0_ in zip(probs, tile(qd), aw, o0, mt, n0):
        qeff_s[ci, gi] = (qd_ - aw_).astype(qeff_s.dtype)
        o0_s[ci, gi] = o0_
        mt_s[ci, gi] = mt_.astype(mt_s.dtype)
        n0_s[ci, gi] = n0_

    for ci in range(nc):
        rs = slice(ci * c, (ci + 1) * c)
        for gi in range(N_GROUPS):
            ls = slice(gi * GROUP_W, (gi + 1) * GROUP_W)
            s = s_ref[gi]
            sb = s.astype(MXU_DTYPE)
            o_s[rs, ls] = jnp.dot(qeff_s[ci, gi], sb, preferred_element_type=F32) + o0_s[ci, gi]
            s_ref[gi] = (s * blk_decay[ci * c:ci * c + 1, ls]
                         - jnp.dot(mt_s[ci, gi], sb, preferred_element_type=F32) + n0_s[ci, gi])

    o = o_s[...]
    o = o * lax.rsqrt(_head_sum(o * o, ones_h, 1) * (1.0 / hd) + RMS_EPS) * nw_ref[...]
    z = z_s[...]
    o_ref[...] = (o * (z * _sigmoid(z))).astype(o_ref.dtype)


def _gdn(h, gain, w, w_gates, conv_w, gate_params, norm_w, seq, tb):
    m, d = h.shape
    nt, n_blocks = seq // tb, m // tb
    h_spec, out_spec, const = _lookahead_specs(n_blocks, tb, d)
    return pl.pallas_call(
        functools.partial(_gdn_kernel, tb=tb, nt=nt),
        grid=(n_blocks + 1,),
        in_specs=[h_spec, const((1, d)), const(w.shape), const(w_gates.shape),
                  const(conv_w.shape), const(gate_params.shape), const(norm_w.shape)],
        out_specs=out_spec(W_AB),
        out_shape=jax.ShapeDtypeStruct((m, W_AB), BF16),
        scratch_shapes=[pltpu.VMEM((tb, w.shape[1] + w_gates.shape[1]), F32), pltpu.VMEM((tb, d), MXU_DTYPE),
                        pltpu.VMEM((tb, W_AB), F32), pltpu.VMEM((SUBLANES + tb, 3 * W_AB), F32)]
                       + _scan_scratch(tb),
        compiler_params=pltpu.CompilerParams(dimension_semantics=("arbitrary",),
                                             vmem_limit_bytes=VMEM_LIMIT),
        name="gdn",
    )(h, gain.reshape(1, d), w, w_gates, conv_w, gate_params, norm_w)


def _rwkv_kernel(h_ref, gain_ref, w_ref, mu_ref, w0_ref, w2_ref, a0_ref, a2_ref, g2_ref, kk_ref, ka_ref,
                 rk_ref, lnw_ref, lnb_ref, o_ref,
                 p_s, u_s, xp_ref, s_ref, qeff_s, y0_s, mt_s, n0_s, y_s, *, tb, nt):
    nh, hd, c = N_HEADS_AB, HEAD_DIM, CHUNK
    nc = tb // c

    @pl.when(pl.program_id(0) == 0)
    def _():
        p_s[...] = jnp.zeros_like(p_s)

    @pl.when(_starts_sequence(nt))
    def _():
        s_ref[...] = jnp.zeros_like(s_ref)
        xp_ref[0:SUBLANES, :] = jnp.zeros((SUBLANES, xp_ref.shape[1]), F32)

    rp = p_s[...]
    fill = _project_ahead(h_ref, gain_ref, [w_ref], u_s, p_s)
    xp_ref[SUBLANES:SUBLANES + tb, :] = rp
    prev = xp_ref[pl.ds(SUBLANES - 1, tb), :]
    xp_ref[0:SUBLANES, :] = rp[tb - SUBLANES:tb, :]
    xs = rp + (prev - rp) * mu_ref[...]
    r, kr, vr = xs[:, 0:W_AB], xs[:, W_AB:2 * W_AB], xs[:, 2 * W_AB:3 * W_AB]
    xwa = xs[:, 3 * W_AB:3 * W_AB + 2 * RWKV_LORA]
    xg = xs[:, 3 * W_AB + 2 * RWKV_LORA:]

    fill()
    tanh_xwa, sig_xg = jnp.tanh(xwa).astype(MXU_DTYPE), _sigmoid(xg).astype(MXU_DTYPE)
    xwa_b = xwa.astype(MXU_DTYPE)
    ones_h = _head_ones(GROUP_W, hd)

    names = ("gate", "k", "rh", "kkh", "bh", "kh", "bt", "kt", "decay")
    parts = {n: [] for n in names}
    for lo in range(0, W_AB, GROUP_W):
        cols = slice(lo, lo + GROUP_W)
        w_lora = jnp.dot(tanh_xwa, w2_ref[:, cols], preferred_element_type=F32)
        lw = -jnp.exp(-_softplus(-(w0_ref[:, cols] + w_lora)) - 0.5)
        a = _sigmoid(a0_ref[:, cols] + jnp.dot(xwa_b, a2_ref[:, cols], preferred_element_type=F32))
        parts["gate"].append(jnp.dot(sig_xg, g2_ref[:, cols], preferred_element_type=F32))
        fill()
        kk = kr[:, cols] * kk_ref[:, cols]
        kk = kk * lax.rsqrt(_sel_r(kk * kk, ones_h, 1) + L2_EPS)
        k = kr[:, cols] * (1.0 + (a - 1.0) * ka_ref[:, cols])
        b = kk * a
        cum = _chunk_cumsum(lw, c, 2)
        last = _chunk_last(cum, tb, c)
        e_neg = jnp.exp(-cum)
        e_rem = jnp.exp(last - cum)
        for n, val in (("k", k), ("rh", r[:, cols] * jnp.exp(cum)), ("kkh", kk * jnp.exp(cum - lw)),
                       ("bh", b * e_neg), ("kh", k * e_neg), ("bt", b * e_rem), ("kt", k * e_rem),
                       ("decay", jnp.exp(last))):
            parts[n].append(val)
        fill()
    whole = {n: jnp.concatenate(v, axis=1) for n, v in parts.items()}
    gate, k, chunk_decay = whole["gate"], whole["k"], whole["decay"]
    rh_all, kkh_all, bh_all, kh_all = whole["rh"], whole["kkh"], whole["bh"], whole["kh"]
    bt_all, kt_all = whole["bt"], whole["kt"]

    mk = _packed_masks()
    blk, causal, strict = mk["blk"], mk["causal"], mk["strict"]

    probs = [(ci, gi) for ci in range(nc) for gi in range(N_GROUPS)]
    tile = lambda x: [x[ci * c:(ci + 1) * c, gi * GROUP_W:(gi + 1) * GROUP_W] for ci, gi in probs]
    rh, kkh, vv, bt, kt = tile(rh_all), tile(kkh_all), tile(vr), tile(bt_all), tile(kt_all)
    lhs = _each(lambda x, y: jnp.concatenate([x, y], axis=0), kkh, rh)
    ab = _each(lambda x, y: _pmm_nt(x, _bd(y, blk)), lhs, tile(bh_all))
    ak = _each(lambda x, y: _pmm_nt(x, _bd(y, blk)), lhs, tile(kh_all))
    a_bb = _each(lambda x: jnp.where(strict, x[:c], 0.0), ab)
    a_rb = _each(lambda x: jnp.where(causal, x[c:], 0.0), ab)
    a_k = _each(lambda x: jnp.concatenate([jnp.where(strict, x[:c], 0.0), jnp.where(causal, x[c:], 0.0)],
                                          axis=0), ak)
    t = _inv_unit_lower(a_bb, mk, fill)
    fill.flush()
    z0 = _each(lambda x, y: _pmm(x, _bd(y, blk)), a_k, vv)
    wk = _each(lambda x, y: _pmm(x, _bd(y, blk)), t, kkh)
    u0 = _each(lambda x, y: -_pmm(x, _bd(y[:c], blk)), t, z0)
    ar = _each(lambda x, y: _pmm(x, _bd(y, blk)), a_rb, wk)
    y0 = _each(lambda x, y, z: _pmm(x, _bd(y, blk)) + z[c:], a_rb, u0, z0)
    mt = _each(lambda x, y: jnp.where(blk, _mm_tn(x, y), 0.0), wk, bt)
    n0 = _each(lambda u_, v_, b_, k_: jnp.where(blk, _mm_tn(jnp.concatenate([u_, v_], axis=0),
                                                            jnp.concatenate([b_, k_], axis=0)), 0.0),
               u0, vv, bt, kt)
    for (ci, gi), rh_, ar_, y0_, mt_, n0_ in zip(probs, rh, ar, y0, mt, n0):
        qeff_s[ci, gi] = (rh_ - ar_).astype(qeff_s.dtype)
        y0_s[ci, gi] = y0_
        mt_s[ci, gi] = mt_.astype(mt_s.dtype)
        n0_s[ci, gi] = n0_

    for ci in range(nc):
        rs = slice(ci * c, (ci + 1) * c)
        for gi in range(N_GROUPS):
            ls = slice(gi * GROUP_W, (gi + 1) * GROUP_W)
            s = s_ref[gi]
            sb = s.astype(MXU_DTYPE)
            y_s[rs, ls] = lax.dot_general(qeff_s[ci, gi], sb, (((1,), (1,)), ((), ())),
                                          preferred_element_type=F32) + y0_s[ci, gi]
            s_ref[gi] = (s * chunk_decay[ci * c:ci * c + 1, ls]
                         - jnp.dot(sb, mt_s[ci, gi], preferred_element_type=F32) + n0_s[ci, gi])

    y = y_s[...]
    inv_hd = 1.0 / hd
    yc = y - _head_sum(y, ones_h, 2) * inv_hd
    yn = yc * lax.rsqrt(_head_sum(yc * yc, ones_h, 1) * inv_hd + RWKV_GN_EPS) * lnw_ref[...] + lnb_ref[...]
    bonus = _head_sum(r * k * rk_ref[...], ones_h, 1) * vr
    o_ref[...] = ((yn + bonus) * gate).astype(o_ref.dtype)


def _rwkv(h, gain, w, params, seq, tb):
    m, d = h.shape
    n_in = w.shape[1]
    nt, n_blocks = seq // tb, m // tb
    h_spec, out_spec, const = _lookahead_specs(n_blocks, tb, d)
    return pl.pallas_call(
        functools.partial(_rwkv_kernel, tb=tb, nt=nt),
        grid=(n_blocks + 1,),
        in_specs=[h_spec, const((1, d)), const(w.shape)] + [const(p.shape) for p in params],
        out_specs=out_spec(W_AB),
        out_shape=jax.ShapeDtypeStruct((m, W_AB), BF16),
        scratch_shapes=[pltpu.VMEM((tb, n_in), F32), pltpu.VMEM((tb, d), MXU_DTYPE),
                        pltpu.VMEM((SUBLANES + tb, n_in), F32)] + _scan_scratch(tb),
        compiler_params=pltpu.CompilerParams(dimension_semantics=("arbitrary",),
                                             vmem_limit_bytes=VMEM_LIMIT),
        name="rwkv7",
    )(h, gain.reshape(1, d), w, *params)


def _rope_table_kernel(angle_ref, cos_ref, sin_ref):
    shape = cos_ref.shape
    pos = _iota2(shape, 0).astype(F32)
    theta = pos * angle_ref[...]
    cos_ref[...] = jnp.cos(theta)
    s = jnp.sin(theta)
    sin_ref[...] = jnp.where(_iota2(shape, 1) < RET_DK // 2, -s, s)


def _rope_tables(seq):
    angle = 1.0 / (ROPE_BASE ** jnp.linspace(0.0, 1.0, RET_DK // 2, dtype=F32))
    angle = jnp.concatenate([angle, angle]).reshape(1, RET_DK)
    return pl.pallas_call(
        _rope_table_kernel,
        out_shape=[jax.ShapeDtypeStruct((seq, RET_DK), F32)] * 2,
        name="rope_tables",
    )(angle)


def _ret_log_gamma():
    return [float(np.log1p(-np.exp2(np.float32(-5.0 - h)))) for h in range(N_HEADS_RET)]


def _ret_kernel(h_ref, gain_ref, wq_ref, wk_ref, wvg_ref, cos_ref, sin_ref, gnw_ref, o_ref,
                p_s, u_s, q_s, k_s, v_s, g_s, s_ref, intra_s, qdec_s, kdec_s, *, tb, nt):
    dk, dv, nh, c = RET_DK, RET_DV, N_HEADS_RET, RET_CHUNK
    wq, wv = nh * dk, nh * dv
    log_gamma = _ret_log_gamma()

    @pl.when(pl.program_id(0) == 0)
    def _():
        p_s[...] = jnp.zeros_like(p_s)
        rel = (_iota2((c, c), 0) - _iota2((c, c), 1)).astype(F32)
        idx = _iota2((c, dk), 0).astype(F32)
        for h, lg in enumerate(log_gamma):
            intra_s[h] = jnp.where(rel >= 0.0, jnp.exp(lg * jnp.maximum(rel, 0.0)), 0.0)
            qdec_s[h] = jnp.exp(lg * (idx + 1.0))
            kdec_s[h] = jnp.exp(lg * (c - 1.0 - idx))

    @pl.when(_starts_sequence(nt))
    def _():
        s_ref[...] = jnp.zeros_like(s_ref)

    fill = _project_ahead(h_ref, gain_ref, [wq_ref, wk_ref, wvg_ref], u_s, p_s)
    cos, sin = cos_ref[...], sin_ref[...]
    for n_slab, lo in enumerate(range(0, p_s.shape[1], GROUP_W)):
        x = p_s[:, lo:lo + GROUP_W]
        if lo < 2 * wq:
            dst, off, scale = (q_s, lo, None) if lo < wq else (k_s, lo - wq, dk ** -0.5)
            for j in range(0, GROUP_W, dk):
                xh = x[:, j:j + dk]
                xr = xh * cos + pltpu.roll(xh, dk // 2, 1) * sin
                dst[:, off + j:off + j + dk] = xr if scale is None else xr * scale
        elif lo < 2 * wq + wv:
            v_s[:, lo - 2 * wq:lo - 2 * wq + GROUP_W] = x.astype(v_s.dtype)
        else:
            g_s[:, lo - 2 * wq - wv:lo - 2 * wq - wv + GROUP_W] = x * _sigmoid(x)
        if n_slab % 3 == 0:
            fill()

    n_points = RET_STEPS_PER_CHUNK * (tb // c)
    n_left = p_s.shape[1] // GROUP_W - (p_s.shape[1] // GROUP_W + 2) // 3
    share = iter([n_left * (k + 1) // n_points - n_left * k // n_points for k in range(n_points)])
    heads = list(range(nh))
    for ci in range(tb // c):
        rows = slice(ci * c, (ci + 1) * c)
        q = [q_s[rows, h * dk:(h + 1) * dk] for h in heads]
        k = [k_s[rows, h * dk:(h + 1) * dk] for h in heads]
        v = [v_s[rows, h * dv:(h + 1) * dv] for h in heads]
        s = [s_ref[h] for h in heads]
        scores = _each(lambda q_, k_, h: _mm_nt(q_, k_) * intra_s[h], q, k, heads)
        fill(next(share))
        cross = _each(lambda q_, s_, h: _mm(q_ * qdec_s[h], s_), q, s, heads)
        fill(next(share))
        inner = _each(_mm, scores, v)
        fill(next(share))
        upd = _each(lambda k_, v_, h: _mm_tn(k_ * kdec_s[h], v_), k, v, heads)
        fill(next(share))
        for h, lg in enumerate(log_gamma):
            s_ref[h] = s[h] * float(np.exp(lg * c)) + upd[h]
        y = _each(jnp.add, inner, cross)
        yc = _each(lambda y_: y_ - jnp.mean(y_, axis=-1, keepdims=True), y)
        fill(next(share))
        yn = _each(lambda y_: y_ * lax.rsqrt(jnp.mean(y_ * y_, axis=-1, keepdims=True) + RET_GN_EPS), yc)
        fill(next(share))
        for h in heads:
            cols = slice(h * dv, (h + 1) * dv)
            o_ref[rows, cols] = (g_s[rows, cols] * yn[h] * gnw_ref[:, cols]).astype(o_ref.dtype)
    fill.flush()


def _retention(h, gain, weights, cos, sin, gn_w, seq, tb):
    m, d = h.shape
    nt, n_blocks = seq // tb, m // tb
    nh, dk, dv, c = N_HEADS_RET, RET_DK, RET_DV, RET_CHUNK
    h_spec, out_spec, const = _lookahead_specs(n_blocks, tb, d)
    pos_blk = pl.BlockSpec((tb, dk), lambda i: (lax.rem(jnp.maximum(i - 1, 0), nt), 0))
    return pl.pallas_call(
        functools.partial(_ret_kernel, tb=tb, nt=nt),
        grid=(n_blocks + 1,),
        in_specs=[h_spec, const((1, d))] + [const(w.shape) for w in weights]
                 + [pos_blk, pos_blk, const(gn_w.shape)],
        out_specs=out_spec(nh * dv),
        out_shape=jax.ShapeDtypeStruct((m, nh * dv), BF16),
        scratch_shapes=[pltpu.VMEM((tb, sum(w.shape[1] for w in weights)), F32), pltpu.VMEM((tb, d), MXU_DTYPE),
                        pltpu.VMEM((tb, nh * dk), F32), pltpu.VMEM((tb, nh * dk), F32),
                        pltpu.VMEM((tb, nh * dv), MXU_DTYPE), pltpu.VMEM((tb, nh * dv), F32),
                        pltpu.VMEM((nh, dk, dv), F32), pltpu.VMEM((nh, c, c), F32),
                        pltpu.VMEM((nh, c, dk), F32), pltpu.VMEM((nh, c, dk), F32)],
        compiler_params=pltpu.CompilerParams(dimension_semantics=("arbitrary",),
                                             vmem_limit_bytes=VMEM_LIMIT),
        name="retention",
    )(h, gain.reshape(1, d), *weights, cos, sin, gn_w)


def _tiles(m, seq):
    assert m % DENSE_ROWS == 0 and seq % MIXER_ROWS == 0, (m, seq)
    return DENSE_ROWS, MIXER_ROWS, MIXER_ROWS


def _row(v):
    return v.reshape(1, -1).astype(F32)


def _even_layer_mix(h, seq, pre_gain, w_in, conv_w, a_log, dt_bias, gdn_norm_w,
                    mu, w0, w2, a0, a2, g2, k_k, k_a, r_k, ln_w, ln_b, w_out):
    _, tb, _ = _tiles(h.shape[0], seq)
    nh = N_HEADS_AB
    w_in = w_in.astype(MXU_DTYPE)
    gate_cols = jnp.pad(w_in[:, 4 * W_AB:4 * W_AB + 2 * nh], ((0, 0), (0, LANES - 2 * nh)))
    pad_g = lambda v: jnp.pad(v.astype(F32), (nh, LANES - 2 * nh))
    gate_params = jnp.stack([pad_g(a_log), pad_g(dt_bias)])
    o_a = _gdn(h, pre_gain, w_in[:, :4 * W_AB], gate_cols, conv_w.astype(F32), gate_params,
               _row(jnp.tile(gdn_norm_w, nh)), seq, tb)

    zeros = jnp.zeros((RWKV_LORA, W_AB), F32)
    params = [_row(mu), _row(w0), jnp.concatenate([w2, zeros]).astype(MXU_DTYPE), _row(a0),
              jnp.concatenate([zeros, a2]).astype(MXU_DTYPE), g2.astype(MXU_DTYPE),
              _row(k_k), _row(k_a), _row(r_k), _row(ln_w), _row(ln_b)]
    o_b = _rwkv(h, pre_gain, w_in[:, 4 * W_AB + 2 * nh:], params, seq, tb)

    w_out = w_out.astype(MXU_DTYPE)
    return [o_a, o_b], [w_out[:W_AB], w_out[W_AB:]]


def _odd_layer_mix(h, seq, pre_gain, w_in, gn_w, w_out):
    _, _, tb = _tiles(h.shape[0], seq)
    d = h.shape[1]
    wv = N_HEADS_RET * RET_DV
    perm = np.concatenate([np.concatenate([np.arange(0, RET_DK, 2), np.arange(1, RET_DK, 2)]) + hh * RET_DK
                           for hh in range(N_HEADS_RET)])
    w_in = w_in.astype(MXU_DTYPE)
    weights = [w_in[:, :d][:, perm], w_in[:, d:2 * d][:, perm], w_in[:, 2 * d:]]
    cos, sin = _rope_tables(seq)
    y = _retention(h, pre_gain, weights, cos, sin, _row(gn_w), seq, tb)
    return [y], [w_out.astype(MXU_DTYPE)]


def kernel(x, norm_mix_pre, norm_mix_post, norm_mlp_pre, norm_mlp_post, mlp_w_up, mlp_w_down, ab_w_in, gdn_conv_w, gdn_a_log, gdn_dt_bias, gdn_norm_w, rwkv_mu, rwkv_w0, rwkv_w2, rwkv_a0, rwkv_a2, rwkv_g2, rwkv_k_k, rwkv_k_a, rwkv_r_k, rwkv_ln_w, rwkv_ln_b, ab_w_out, ret_w_in, ret_gn_w, ret_w_out):
    batch, seq, d = x.shape
    h = x.astype(F32).reshape(batch * seq, d)
    tm = _tiles(batch * seq, seq)[0]
    for layer in range(norm_mix_pre.shape[0]):
        j = layer // 2
        if layer % 2 == 0:
            acts, w_outs = _even_layer_mix(h, seq, norm_mix_pre[layer], ab_w_in[j], gdn_conv_w[j],
                                           gdn_a_log[j], gdn_dt_bias[j], gdn_norm_w[j], rwkv_mu[j], rwkv_w0[j],
                                           rwkv_w2[j], rwkv_a0[j], rwkv_a2[j], rwkv_g2[j], rwkv_k_k[j],
                                           rwkv_k_a[j], rwkv_r_k[j], rwkv_ln_w[j], rwkv_ln_b[j], ab_w_out[j])
        else:
            acts, w_outs = _odd_layer_mix(h, seq, norm_mix_pre[layer], ret_w_in[j], ret_gn_w[j], ret_w_out[j])
        h = _mix_mlp(acts, w_outs, h, norm_mix_post[layer], norm_mlp_pre[layer],
                     mlp_w_up[layer].astype(MXU_DTYPE), mlp_w_down[layer].astype(MXU_DTYPE),
                     norm_mlp_post[layer], tm, MLP_FF_BLOCK)
    return h.reshape(batch, seq, d).astype(x.dtype)
```

```python
import functools

import numpy as np
import jax
import jax.numpy as jnp
from jax import lax
from jax.experimental import pallas as pl
from jax.experimental.pallas import tpu as pltpu

F32 = jnp.float32
BF16 = jnp.bfloat16
MXU_DTYPE = BF16

RMS_EPS = 1e-6
L2_EPS = 1e-6
HEAD_DIM = 64
N_HEADS_AB = 8
W_AB = N_HEADS_AB * HEAD_DIM
CHUNK = 64
SUB = 16
CONV_WIDTH = 4
RWKV_GN_EPS = 64e-5
RWKV_LORA = 64
RWKV_GATE_LORA = 128
N_HEADS_RET = 8
RET_DK = 128
RET_DV = 256
RET_CHUNK = 128
RET_GN_EPS = 1e-6
ROPE_BASE = 10000.0
LANES = 128
SUBLANES = 8
VMEM_LIMIT = 56 * 1024 * 1024
DENSE_ROWS = 512
MIXER_ROWS = 512
MLP_FF_BLOCK = 1024
RET_STEPS_PER_CHUNK = 6


def _mm(a, b):
    return jnp.dot(a.astype(MXU_DTYPE), b.astype(MXU_DTYPE), preferred_element_type=F32)


def _mm_nt(a, b):
    return lax.dot_general(a.astype(MXU_DTYPE), b.astype(MXU_DTYPE), (((1,), (1,)), ((), ())),
                           preferred_element_type=F32)


def _mm_tn(a, b):
    return lax.dot_general(a.astype(MXU_DTYPE), b.astype(MXU_DTYPE), (((0,), (0,)), ((), ())),
                           preferred_element_type=F32)


def _split(x, n):
    parts, r = [], x
    for i in range(n):
        p = r.astype(BF16)
        parts.append(p)
        if i + 1 < n:
            r = r - p.astype(F32)
    return parts


def _sel_r(x, e, n):
    acc = None
    for p in _split(x, n):
        d = jnp.dot(p, e, preferred_element_type=F32)
        acc = d if acc is None else acc + d
    return acc


def _sel_l(e, x, n):
    acc = None
    for p in _split(x, n):
        d = jnp.dot(e, p, preferred_element_type=F32)
        acc = d if acc is None else acc + d
    return acc


def _iota2(shape, dim):
    return lax.broadcasted_iota(jnp.int32, shape, dim)


def _head_ones(width, head):
    r, c = _iota2((width, width), 0), _iota2((width, width), 1)
    return jnp.where((r // head) == (c // head), 1.0, 0.0).astype(BF16)


def _head_sum(x, ones_g, n):
    w = ones_g.shape[0]
    return jnp.concatenate([_sel_r(x[:, i:i + w], ones_g, n) for i in range(0, x.shape[1], w)], axis=1)


def _chunk_tril(tb, chunk):
    r, c = _iota2((tb, tb), 0), _iota2((tb, tb), 1)
    return jnp.where(((r // chunk) == (c // chunk)) & (c <= r), 1.0, 0.0).astype(BF16)


def _chunk_cumsum(x, chunk, n):
    rows = min(x.shape[0], GROUP_W)
    tril = _chunk_tril(rows, chunk)
    return jnp.concatenate([_sel_l(tril, x[r:r + rows], n) for r in range(0, x.shape[0], rows)], axis=0)


def _chunk_last(x, tb, chunk):
    return jnp.concatenate(
        [jnp.broadcast_to(x[(c + 1) * chunk - 1:(c + 1) * chunk, :], (chunk, x.shape[1]))
         for c in range(tb // chunk)], axis=0)


def _sigmoid(x):
    return 1.0 / (1.0 + jnp.exp(-x))


def _softplus(x):
    return jnp.maximum(x, 0.0) + jnp.log1p(jnp.exp(-jnp.abs(x)))


GROUP = 4
GROUP_W = GROUP * HEAD_DIM
N_GROUPS = N_HEADS_AB // GROUP


def _packed_masks():
    c = CHUNK
    r, l = _iota2((c, GROUP_W), 0), _iota2((c, GROUP_W), 1) % c
    rr, cc = _iota2((GROUP_W, GROUP_W), 0), _iota2((GROUP_W, GROUP_W), 1)
    return dict(causal=r >= l, strict=r > l, eye=jnp.where(r == l, 1.0, 0.0).astype(F32),
                same_sub=(r // SUB) == (l // SUB), blk=(rr // c) == (cc // c))


def _bd(x, blk):
    xb = x.astype(MXU_DTYPE)
    return jnp.where(blk, jnp.concatenate([xb] * GROUP, axis=0), jnp.zeros((), MXU_DTYPE))


def _pmm(a, ybd):
    return jnp.dot(a.astype(MXU_DTYPE), ybd, preferred_element_type=F32)


def _pmm_nt(a, ybd):
    return lax.dot_general(a.astype(MXU_DTYPE), ybd, (((1,), (1,)), ((), ())), preferred_element_type=F32)


def _each(f, *seqs):
    return [f(*args) for args in zip(*seqs)]


class _Interleave:
    def __init__(self, items):
        self._items = list(items)

    def __call__(self, n=1):
        for _ in range(min(n, len(self._items))):
            self._items.pop(0)()

    def flush(self):
        self(len(self._items))


def _project_ahead(h_ref, gain_ref, w_refs, u_s, p_s):
    u_s[...] = _rms(h_ref[...], gain_ref[...]).astype(u_s.dtype)

    def item(w_ref, lo, hi, base):
        def run():
            p_s[:, base + lo:base + hi] = jnp.dot(u_s[...], w_ref[:, lo:hi], preferred_element_type=F32)
        return run

    items, base = [], 0
    for w_ref in w_refs:
        n = w_ref.shape[1]
        items += [item(w_ref, lo, min(lo + GROUP_W, n), base) for lo in range(0, n, GROUP_W)]
        base += n
    return _Interleave(items)


def _lookahead_specs(n_blocks, tb, d):
    h_spec = pl.BlockSpec((tb, d), lambda i: (jnp.minimum(i, n_blocks - 1), 0))
    out_spec = lambda n: pl.BlockSpec((tb, n), lambda i: (jnp.maximum(i - 1, 0), 0))
    const = lambda shape: pl.BlockSpec(shape, lambda i: (0,) * len(shape), pipeline_mode=pl.Buffered(1))
    return h_spec, out_spec, const


def _starts_sequence(nt):
    i = pl.program_id(0)
    return (i == 0) | (lax.rem(i + nt - 1, nt) == 0)


def _inv_unit_lower(a, mk, fill):
    c = CHUNK

    def mm(xs, ys):
        out = _each(lambda x, y: _pmm(x, _bd(y, mk["blk"])), xs, ys)
        fill()
        return out

    def mm2(xs, zs, ys):
        out = mm(_each(lambda x, z: jnp.concatenate([x, z], axis=0), xs, zs), ys)
        return [o[:c] for o in out], [o[c:] for o in out]

    add = lambda xs, ys: _each(jnp.add, xs, ys)
    d = _each(lambda x: jnp.where(mk["same_sub"], x, 0.0), a)
    l = _each(jnp.subtract, a, d)
    d2 = mm(d, d)
    p = _each(lambda x: mk["eye"] - x, d)
    pd, d4 = mm2(p, d2, d2)
    p = add(p, pd)
    pd, d8 = mm2(p, d4, d4)
    p = add(p, pd)
    td = add(p, mm(p, d8))
    n = mm(l, td)
    n2, tn = mm2(n, td, n)
    y = _each(jnp.subtract, td, tn)
    return add(y, mm(y, n2))


def _scan_scratch(tb):
    nc = tb // CHUNK
    per = lambda rows, dt: pltpu.VMEM((nc, N_GROUPS, rows, GROUP_W), dt)
    return [pltpu.VMEM((N_GROUPS, GROUP_W, GROUP_W), F32), per(CHUNK, MXU_DTYPE), per(CHUNK, F32),
            per(GROUP_W, MXU_DTYPE), per(GROUP_W, F32), pltpu.VMEM((tb, W_AB), F32)]


def _rms(x, g):
    return x * lax.rsqrt(jnp.mean(x * x, axis=-1, keepdims=True) + RMS_EPS) * g


def _mix_mlp_kernel(*refs, n_in, ff_blk):
    a_refs, w_refs = refs[:n_in], refs[n_in:2 * n_in]
    h_ref, gmix_ref, g1_ref, wu_ref, wd_ref, g2_ref, o_ref = refs[2 * n_in:]
    mix = None
    for a_ref, w_ref in zip(a_refs, w_refs):
        d = jnp.dot(a_ref[...], w_ref[...], preferred_element_type=F32)
        mix = d if mix is None else mix + d
    x = h_ref[...] + _rms(mix, gmix_ref[...])
    u = _rms(x, g1_ref[...]).astype(MXU_DTYPE)
    acc = None
    for j in range(wu_ref.shape[1] // ff_blk):
        a = jnp.dot(u, wu_ref[:, j * ff_blk:(j + 1) * ff_blk], preferred_element_type=F32)
        a = jnp.square(jnp.maximum(a, 0.0)).astype(MXU_DTYPE)
        d = jnp.dot(a, wd_ref[j * ff_blk:(j + 1) * ff_blk, :], preferred_element_type=F32)
        acc = d if acc is None else acc + d
    o_ref[...] = x + _rms(acc, g2_ref[...])


def _mix_mlp(acts, w_outs, h, g_mix, g1, w_up, w_down, g2, tm, ff_blk):
    m, d = h.shape
    n_in = len(acts)
    rows = lambda n: pl.BlockSpec((tm, n), lambda i: (i, 0))
    resident = lambda shape: pl.BlockSpec(shape, lambda i: (0, 0), pipeline_mode=pl.Buffered(1))
    gains = [g.reshape(1, d) for g in (g_mix, g1, g2)]
    return pl.pallas_call(
        functools.partial(_mix_mlp_kernel, n_in=n_in, ff_blk=ff_blk),
        grid=(m // tm,),
        in_specs=[rows(a.shape[1]) for a in acts] + [resident(w.shape) for w in w_outs]
                 + [rows(d), resident((1, d)), resident((1, d)),
                    resident(w_up.shape), resident(w_down.shape), resident((1, d))],
        out_specs=rows(d),
        out_shape=jax.ShapeDtypeStruct((m, d), F32),
        compiler_params=pltpu.CompilerParams(dimension_semantics=("parallel",),
                                             vmem_limit_bytes=VMEM_LIMIT),
        name="mix_mlp",
    )(*acts, *w_outs, h, gains[0], gains[1], w_up, w_down, gains[2])


def _gdn_kernel(h_ref, gain_ref, w_ref, cw_ref, gp_ref, nw_ref, o_ref,
                p_s, u_s, z_s, xp_ref, s_ref, qeff_s, o0_s, mt_s, n0_s, o_s, *, tb, nt):
    nh, hd, c = N_HEADS_AB, HEAD_DIM, CHUNK
    nc = tb // c

    @pl.when(pl.program_id(0) == 0)
    def _():
        p_s[...] = jnp.zeros_like(p_s)

    @pl.when(_starts_sequence(nt))
    def _():
        s_ref[...] = jnp.zeros_like(s_ref)
        xp_ref[0:SUBLANES, :] = jnp.zeros((SUBLANES, xp_ref.shape[1]), F32)

    xp_ref[SUBLANES:SUBLANES + tb, :] = p_s[:, 0:3 * W_AB]
    z_s[...] = p_s[:, 3 * W_AB:4 * W_AB]
    gt = p_s[:, 4 * W_AB:]
    fill = _project_ahead(h_ref, gain_ref, [w_ref], u_s, p_s)

    cw = cw_ref[...]
    ones_h = _head_ones(GROUP_W, hd)
    slabs = []
    for lo in range(0, 3 * W_AB, GROUP_W):
        cols = slice(lo, lo + GROUP_W)
        y = xp_ref[SUBLANES:SUBLANES + tb, cols] * cw[CONV_WIDTH - 1:CONV_WIDTH, cols]
        for j in range(CONV_WIDTH - 1):
            y = y + xp_ref[pl.ds(SUBLANES - (CONV_WIDTH - 1) + j, tb), cols] * cw[j:j + 1, cols]
        y = y * _sigmoid(y)
        if lo < 2 * W_AB:
            y = y * lax.rsqrt(_sel_r(y * y, ones_h, 1) + L2_EPS)
        slabs.append(y)
        fill()
    xp_ref[0:SUBLANES, :] = xp_ref[tb:tb + SUBLANES, :]
    per = W_AB // GROUP_W
    qn = jnp.concatenate(slabs[0:per], axis=1) * (hd ** -0.5)
    kn = jnp.concatenate(slabs[per:2 * per], axis=1)
    v = jnp.concatenate(slabs[2 * per:], axis=1)

    gp = gp_ref[...]
    beta = _sigmoid(gt)
    g = -jnp.exp(gp[0:1, :]) * _softplus(gt + gp[1:2, :])
    gcum = _chunk_cumsum(g, c, 3)
    r_, c_ = _iota2((LANES, W_AB), 0), _iota2((LANES, W_AB), 1)
    pick_beta = jnp.where(r_ == c_ // hd, 1.0, 0.0).astype(BF16)
    pick_g = jnp.where(r_ == nh + c_ // hd, 1.0, 0.0).astype(BF16)
    beta_e = _sel_r(beta, pick_beta, 2)
    gc_e = _sel_r(gcum, pick_g, 3)
    glast_e = _chunk_last(gc_e, tb, c)
    gct = gcum.T
    fill()

    eg = jnp.exp(gc_e)
    kb = kn * beta_e
    kbe, vb, qd = kb * eg, v * beta_e, qn * eg
    fill()
    kd = kn * jnp.exp(glast_e - gc_e)
    blk_decay = jnp.exp(glast_e)
    fill()

    mk = _packed_masks()
    blk = mk["blk"]

    probs = [(ci, gi) for ci in range(nc) for gi in range(N_GROUPS)]
    tile = lambda x: [x[ci * c:(ci + 1) * c, gi * GROUP_W:(gi + 1) * GROUP_W] for ci, gi in probs]
    g_rows = [jnp.concatenate([gct[nh + h:nh + h + 1, ci * c:(ci + 1) * c] for h in range(nh)], axis=1)
              for ci in range(nc)]
    diff = _each(lambda gcol, p: gcol - g_rows[p[0]][:, p[1] * GROUP_W:(p[1] + 1) * GROUP_W], tile(gc_e), probs)
    dec = _each(lambda x: jnp.where(mk["causal"], jnp.exp(jnp.where(mk["causal"], x, 0.0)), 0.0), diff)
    both = _each(lambda kb_, q_, k_: _pmm_nt(jnp.concatenate([kb_, q_], axis=0), _bd(k_, blk)),
                 tile(kb), tile(qn), tile(kn))
    a = _each(lambda x, e: jnp.where(mk["strict"], x[:c] * e, 0.0), both, dec)
    attn = _each(lambda x, e: x[c:] * e, both, dec)
    t = _inv_unit_lower(a, mk, fill)
    fill.flush()
    w = _each(lambda t_, x: _pmm(t_, _bd(x, blk)), t, tile(kbe))
    u = _each(lambda t_, x: _pmm(t_, _bd(x, blk)), t, tile(vb))
    aw = _each(lambda at, x: _pmm(at, _bd(x, blk)), attn, w)
    o0 = _each(lambda at, x: _pmm(at, _bd(x, blk)), attn, u)
    mt = _each(lambda kd_, x: jnp.where(blk, _mm_tn(kd_, x), 0.0), tile(kd), w)
    n0 = _each(lambda kd_, x: jnp.where(blk, _mm_tn(kd_, x), 0.0), tile(kd), u)
    for (ci, gi), qd_, aw_, o0_, mt_, n0_ in zip(probs, tile(qd), aw, o0, mt, n0):
        qeff_s[ci, gi] = (qd_ - aw_).astype(qeff_s.dtype)
        o0_s[ci, gi] = o0_
        mt_s[ci, gi] = mt_.astype(mt_s.dtype)
        n0_s[ci, gi] = n0_

    for ci in range(nc):
        rs = slice(ci * c, (ci + 1) * c)
        for gi in range(N_GROUPS):
            ls = slice(gi * GROUP_W, (gi + 1) * GROUP_W)
            s = s_ref[gi]
            sb = s.astype(MXU_DTYPE)
            o_s[rs, ls] = jnp.dot(qeff_s[ci, gi], sb, preferred_element_type=F32) + o0_s[ci, gi]
            s_ref[gi] = (s * blk_decay[ci * c:ci * c + 1, ls]
                         - jnp.dot(mt_s[ci, gi], sb, preferred_element_type=F32) + n0_s[ci, gi])

    o = o_s[...]
    o = o * lax.rsqrt(_head_sum(o * o, ones_h, 1) * (1.0 / hd) + RMS_EPS) * nw_ref[...]
    z = z_s[...]
    o_ref[...] = (o * (z * _sigmoid(z))).astype(o_ref.dtype)


def _gdn(h, gain, w, conv_w, gate_params, norm_w, seq, tb):
    m, d = h.shape
    nt, n_blocks = seq // tb, m // tb
    h_spec, out_spec, const = _lookahead_specs(n_blocks, tb, d)
    return pl.pallas_call(
        functools.partial(_gdn_kernel, tb=tb, nt=nt),
        grid=(n_blocks + 1,),
        in_specs=[h_spec, const((1, d)), const(w.shape),
                  const(conv_w.shape), const(gate_params.shape), const(norm_w.shape)],
        out_specs=out_spec(W_AB),
        out_shape=jax.ShapeDtypeStruct((m, W_AB), BF16),
        scratch_shapes=[pltpu.VMEM((tb, w.shape[1]), F32), pltpu.VMEM((tb, d), MXU_DTYPE),
                        pltpu.VMEM((tb, W_AB), F32), pltpu.VMEM((SUBLANES + tb, 3 * W_AB), F32)]
                       + _scan_scratch(tb),
        compiler_params=pltpu.CompilerParams(dimension_semantics=("arbitrary",),
                                             vmem_limit_bytes=VMEM_LIMIT),
        name="gdn",
    )(h, gain.reshape(1, d), w, conv_w, gate_params, norm_w)


def _rwkv_kernel(h_ref, gain_ref, w_ref, mu_ref, w0_ref, w2_ref, a0_ref, a2_ref, g2_ref, kk_ref, ka_ref,
                 rk_ref, lnw_ref, lnb_ref, o_ref,
                 p_s, u_s, xp_ref, s_ref, qeff_s, y0_s, mt_s, n0_s, y_s, *, tb, nt):
    nh, hd, c = N_HEADS_AB, HEAD_DIM, CHUNK
    nc = tb // c

    @pl.when(pl.program_id(0) == 0)
    def _():
        p_s[...] = jnp.zeros_like(p_s)

    @pl.when(_starts_sequence(nt))
    def _():
        s_ref[...] = jnp.zeros_like(s_ref)
        xp_ref[0:SUBLANES, :] = jnp.zeros((SUBLANES, xp_ref.shape[1]), F32)

    rp = p_s[...]
    fill = _project_ahead(h_ref, gain_ref, [w_ref], u_s, p_s)
    xp_ref[SUBLANES:SUBLANES + tb, :] = rp
    prev = xp_ref[pl.ds(SUBLANES - 1, tb), :]
    xp_ref[0:SUBLANES, :] = rp[tb - SUBLANES:tb, :]
    xs = rp + (prev - rp) * mu_ref[...]
    r, kr, vr = xs[:, 0:W_AB], xs[:, W_AB:2 * W_AB], xs[:, 2 * W_AB:3 * W_AB]
    xwa = xs[:, 3 * W_AB:3 * W_AB + 2 * RWKV_LORA]
    xg = xs[:, 3 * W_AB + 2 * RWKV_LORA:]

    fill()
    tanh_xwa, sig_xg = jnp.tanh(xwa).astype(MXU_DTYPE), _sigmoid(xg).astype(MXU_DTYPE)
    xwa_b = xwa.astype(MXU_DTYPE)
    ones_h = _head_ones(GROUP_W, hd)

    names = ("gate", "k", "rh", "kkh", "bh", "kh", "bt", "kt", "decay")
    parts = {n: [] for n in names}
    for lo in range(0, W_AB, GROUP_W):
        cols = slice(lo, lo + GROUP_W)
        w_lora = jnp.dot(tanh_xwa, w2_ref[:, cols], preferred_element_type=F32)
        lw = -jnp.exp(-_softplus(-(w0_ref[:, cols] + w_lora)) - 0.5)
        a = _sigmoid(a0_ref[:, cols] + jnp.dot(xwa_b, a2_ref[:, cols], preferred_element_type=F32))
        parts["gate"].append(jnp.dot(sig_xg, g2_ref[:, cols], preferred_element_type=F32))
        fill()
        kk = kr[:, cols] * kk_ref[:, cols]
        kk = kk * lax.rsqrt(_sel_r(kk * kk, ones_h, 1) + L2_EPS)
        k = kr[:, cols] * (1.0 + (a - 1.0) * ka_ref[:, cols])
        b = kk * a
        cum = _chunk_cumsum(lw, c, 2)
        last = _chunk_last(cum, tb, c)
        e_neg = jnp.exp(-cum)
        e_rem = jnp.exp(last - cum)
        for n, val in (("k", k), ("rh", r[:, cols] * jnp.exp(cum)), ("kkh", kk * jnp.exp(cum - lw)),
                       ("bh", b * e_neg), ("kh", k * e_neg), ("bt", b * e_rem), ("kt", k * e_rem),
                       ("decay", jnp.exp(last))):
            parts[n].append(val)
        fill()
    whole = {n: jnp.concatenate(v, axis=1) for n, v in parts.items()}
    gate, k, chunk_decay = whole["gate"], whole["k"], whole["decay"]
    rh_all, kkh_all, bh_all, kh_all = whole["rh"], whole["kkh"], whole["bh"], whole["kh"]
    bt_all, kt_all = whole["bt"], whole["kt"]

    mk = _packed_masks()
    blk, causal, strict = mk["blk"], mk["causal"], mk["strict"]

    probs = [(ci, gi) for ci in range(nc) for gi in range(N_GROUPS)]
    tile = lambda x: [x[ci * c:(ci + 1) * c, gi * GROUP_W:(gi + 1) * GROUP_W] for ci, gi in probs]
    rh, kkh, vv, bt, kt = tile(rh_all), tile(kkh_all), tile(vr), tile(bt_all), tile(kt_all)
    lhs = _each(lambda x, y: jnp.concatenate([x, y], axis=0), kkh, rh)
    ab = _each(lambda x, y: _pmm_nt(x, _bd(y, blk)), lhs, tile(bh_all))
    ak = _each(lambda x, y: _pmm_nt(x, _bd(y, blk)), lhs, tile(kh_all))
    a_bb = _each(lambda x: jnp.where(strict, x[:c], 0.0), ab)
    a_rb = _each(lambda x: jnp.where(causal, x[c:], 0.0), ab)
    a_k = _each(lambda x: jnp.concatenate([jnp.where(strict, x[:c], 0.0), jnp.where(causal, x[c:], 0.0)],
                                          axis=0), ak)
    t = _inv_unit_lower(a_bb, mk, fill)
    fill.flush()
    z0 = _each(lambda x, y: _pmm(x, _bd(y, blk)), a_k, vv)
    wk = _each(lambda x, y: _pmm(x, _bd(y, blk)), t, kkh)
    u0 = _each(lambda x, y: -_pmm(x, _bd(y[:c], blk)), t, z0)
    ar = _each(lambda x, y: _pmm(x, _bd(y, blk)), a_rb, wk)
    y0 = _each(lambda x, y, z: _pmm(x, _bd(y, blk)) + z[c:], a_rb, u0, z0)
    mt = _each(lambda x, y: jnp.where(blk, _mm_tn(x, y), 0.0), wk, bt)
    n0 = _each(lambda u_, v_, b_, k_: jnp.where(blk, _mm_tn(jnp.concatenate([u_, v_], axis=0),
                                                            jnp.concatenate([b_, k_], axis=0)), 0.0),
               u0, vv, bt, kt)
    for (ci, gi), rh_, ar_, y0_, mt_, n0_ in zip(probs, rh, ar, y0, mt, n0):
        qeff_s[ci, gi] = (rh_ - ar_).astype(qeff_s.dtype)
        y0_s[ci, gi] = y0_
        mt_s[ci, gi] = mt_.astype(mt_s.dtype)
        n0_s[ci, gi] = n0_

    for ci in range(nc):
        rs = slice(ci * c, (ci + 1) * c)
        for gi in range(N_GROUPS):
            ls = slice(gi * GROUP_W, (gi + 1) * GROUP_W)
            s = s_ref[gi]
            sb = s.astype(MXU_DTYPE)
            y_s[rs, ls] = lax.dot_general(qeff_s[ci, gi], sb, (((1,), (1,)), ((), ())),
                                          preferred_element_type=F32) + y0_s[ci, gi]
            s_ref[gi] = (s * chunk_decay[ci * c:ci * c + 1, ls]
                         - jnp.dot(sb, mt_s[ci, gi], preferred_element_type=F32) + n0_s[ci, gi])

    y = y_s[...]
    inv_hd = 1.0 / hd
    yc = y - _head_sum(y, ones_h, 2) * inv_hd
    yn = yc * lax.rsqrt(_head_sum(yc * yc, ones_h, 1) * inv_hd + RWKV_GN_EPS) * lnw_ref[...] + lnb_ref[...]
    bonus = _head_sum(r * k * rk_ref[...], ones_h, 1) * vr
    o_ref[...] = ((yn + bonus) * gate).astype(o_ref.dtype)


def _rwkv(h, gain, w, params, seq, tb):
    m, d = h.shape
    n_in = w.shape[1]
    nt, n_blocks = seq // tb, m // tb
    h_spec, out_spec, const = _lookahead_specs(n_blocks, tb, d)
    return pl.pallas_call(
        functools.partial(_rwkv_kernel, tb=tb, nt=nt),
        grid=(n_blocks + 1,),
        in_specs=[h_spec, const((1, d)), const(w.shape)] + [const(p.shape) for p in params],
        out_specs=out_spec(W_AB),
        out_shape=jax.ShapeDtypeStruct((m, W_AB), BF16),
        scratch_shapes=[pltpu.VMEM((tb, n_in), F32), pltpu.VMEM((tb, d), MXU_DTYPE),
                        pltpu.VMEM((SUBLANES + tb, n_in), F32)] + _scan_scratch(tb),
        compiler_params=pltpu.CompilerParams(dimension_semantics=("arbitrary",),
                                             vmem_limit_bytes=VMEM_LIMIT),
        name="rwkv7",
    )(h, gain.reshape(1, d), w, *params)


def _rope_table_kernel(angle_ref, cos_ref, sin_ref):
    shape = cos_ref.shape
    pos = _iota2(shape, 0).astype(F32)
    theta = pos * angle_ref[...]
    cos_ref[...] = jnp.cos(theta)
    s = jnp.sin(theta)
    sin_ref[...] = jnp.where(_iota2(shape, 1) < RET_DK // 2, -s, s)


def _rope_tables(seq):
    angle = 1.0 / (ROPE_BASE ** jnp.linspace(0.0, 1.0, RET_DK // 2, dtype=F32))
    angle = jnp.concatenate([angle, angle]).reshape(1, RET_DK)
    return pl.pallas_call(
        _rope_table_kernel,
        out_shape=[jax.ShapeDtypeStruct((seq, RET_DK), F32)] * 2,
        name="rope_tables",
    )(angle)


def _ret_log_gamma():
    return [float(np.log1p(-np.exp2(np.float32(-5.0 - h)))) for h in range(N_HEADS_RET)]


def _ret_kernel(h_ref, gain_ref, wq_ref, wk_ref, wvg_ref, cos_ref, sin_ref, gnw_ref, o_ref,
                p_s, u_s, q_s, k_s, v_s, g_s, s_ref, intra_s, qdec_s, kdec_s, *, tb, nt):
    dk, dv, nh, c = RET_DK, RET_DV, N_HEADS_RET, RET_CHUNK
    wq, wv = nh * dk, nh * dv
    log_gamma = _ret_log_gamma()

    @pl.when(pl.program_id(0) == 0)
    def _():
        p_s[...] = jnp.zeros_like(p_s)
        rel = (_iota2((c, c), 0) - _iota2((c, c), 1)).astype(F32)
        idx = _iota2((c, dk), 0).astype(F32)
        for h, lg in enumerate(log_gamma):
            intra_s[h] = jnp.where(rel >= 0.0, jnp.exp(lg * jnp.maximum(rel, 0.0)), 0.0)
            qdec_s[h] = jnp.exp(lg * (idx + 1.0))
            kdec_s[h] = jnp.exp(lg * (c - 1.0 - idx))

    @pl.when(_starts_sequence(nt))
    def _():
        s_ref[...] = jnp.zeros_like(s_ref)

    fill = _project_ahead(h_ref, gain_ref, [wq_ref, wk_ref, wvg_ref], u_s, p_s)
    cos, sin = cos_ref[...], sin_ref[...]
    for n_slab, lo in enumerate(range(0, p_s.shape[1], GROUP_W)):
        x = p_s[:, lo:lo + GROUP_W]
        if lo < 2 * wq:
            dst, off, scale = (q_s, lo, None) if lo < wq else (k_s, lo - wq, dk ** -0.5)
            for j in range(0, GROUP_W, dk):
                xh = x[:, j:j + dk]
                xr = xh * cos + pltpu.roll(xh, dk // 2, 1) * sin
                dst[:, off + j:off + j + dk] = xr if scale is None else xr * scale
        elif lo < 2 * wq + wv:
            v_s[:, lo - 2 * wq:lo - 2 * wq + GROUP_W] = x.astype(v_s.dtype)
        else:
            g_s[:, lo - 2 * wq - wv:lo - 2 * wq - wv + GROUP_W] = x * _sigmoid(x)
        if n_slab % 3 == 0:
            fill()

    n_points = RET_STEPS_PER_CHUNK * (tb // c)
    n_left = p_s.shape[1] // GROUP_W - (p_s.shape[1] // GROUP_W + 2) // 3
    share = iter([n_left * (k + 1) // n_points - n_left * k // n_points for k in range(n_points)])
    heads = list(range(nh))
    for ci in range(tb // c):
        rows = slice(ci * c, (ci + 1) * c)
        q = [q_s[rows, h * dk:(h + 1) * dk] for h in heads]
        k = [k_s[rows, h * dk:(h + 1) * dk] for h in heads]
        v = [v_s[rows, h * dv:(h + 1) * dv] for h in heads]
        s = [s_ref[h] for h in heads]
        scores = _each(lambda q_, k_, h: _mm_nt(q_, k_) * intra_s[h], q, k, heads)
        fill(next(share))
        cross = _each(lambda q_, s_, h: _mm(q_ * qdec_s[h], s_), q, s, heads)
        fill(next(share))
        inner = _each(_mm, scores, v)
        fill(next(share))
        upd = _each(lambda k_, v_, h: _mm_tn(k_ * kdec_s[h], v_), k, v, heads)
        fill(next(share))
        for h, lg in enumerate(log_gamma):
            s_ref[h] = s[h] * float(np.exp(lg * c)) + upd[h]
        y = _each(jnp.add, inner, cross)
        yc = _each(lambda y_: y_ - jnp.mean(y_, axis=-1, keepdims=True), y)
        fill(next(share))
        yn = _each(lambda y_: y_ * lax.rsqrt(jnp.mean(y_ * y_, axis=-1, keepdims=True) + RET_GN_EPS), yc)
        fill(next(share))
        for h in heads:
            cols = slice(h * dv, (h + 1) * dv)
            o_ref[rows, cols] = (g_s[rows, cols] * yn[h] * gnw_ref[:, cols]).astype(o_ref.dtype)
    fill.flush()


def _retention(h, gain, weights, cos, sin, gn_w, seq, tb):
    m, d = h.shape
    nt, n_blocks = seq // tb, m // tb
    nh, dk, dv, c = N_HEADS_RET, RET_DK, RET_DV, RET_CHUNK
    h_spec, out_spec, const = _lookahead_specs(n_blocks, tb, d)
    pos_blk = pl.BlockSpec((tb, dk), lambda i: (lax.rem(jnp.maximum(i - 1, 0), nt), 0))
    return pl.pallas_call(
        functools.partial(_ret_kernel, tb=tb, nt=nt),
        grid=(n_blocks + 1,),
        in_specs=[h_spec, const((1, d))] + [const(w.shape) for w in weights]
                 + [pos_blk, pos_blk, const(gn_w.shape)],
        out_specs=out_spec(nh * dv),
        out_shape=jax.ShapeDtypeStruct((m, nh * dv), BF16),
        scratch_shapes=[pltpu.VMEM((tb, sum(w.shape[1] for w in weights)), F32), pltpu.VMEM((tb, d), MXU_DTYPE),
                        pltpu.VMEM((tb, nh * dk), F32), pltpu.VMEM((tb, nh * dk), F32),
                        pltpu.VMEM((tb, nh * dv), MXU_DTYPE), pltpu.VMEM((tb, nh * dv), F32),
                        pltpu.VMEM((nh, dk, dv), F32), pltpu.VMEM((nh, c, c), F32),
                        pltpu.VMEM((nh, c, dk), F32), pltpu.VMEM((nh, c, dk), F32)],
        compiler_params=pltpu.CompilerParams(dimension_semantics=("arbitrary",),
                                             vmem_limit_bytes=VMEM_LIMIT),
        name="retention",
    )(h, gain.reshape(1, d), *weights, cos, sin, gn_w)


def _tiles(m, seq):
    assert m % DENSE_ROWS == 0 and seq % MIXER_ROWS == 0, (m, seq)
    return DENSE_ROWS, MIXER_ROWS, MIXER_ROWS


def _row(v):
    return v.reshape(1, -1).astype(F32)


def _even_layer_mix(h, seq, pre_gain, w_in, conv_w, a_log, dt_bias, gdn_norm_w,
                    mu, w0, w2, a0, a2, g2, k_k, k_a, r_k, ln_w, ln_b, w_out):
    _, tb, _ = _tiles(h.shape[0], seq)
    nh = N_HEADS_AB
    w_in = w_in.astype(MXU_DTYPE)
    gate_cols = jnp.pad(w_in[:, 4 * W_AB:4 * W_AB + 2 * nh], ((0, 0), (0, LANES - 2 * nh)))
    w_gdn = jnp.concatenate([w_in[:, :4 * W_AB], gate_cols], axis=1)
    pad_g = lambda v: jnp.pad(v.astype(F32), (nh, LANES - 2 * nh))
    gate_params = jnp.stack([pad_g(a_log), pad_g(dt_bias)])
    o_a = _gdn(h, pre_gain, w_gdn, conv_w.astype(F32), gate_params,
               _row(jnp.tile(gdn_norm_w, nh)), seq, tb)

    zeros = jnp.zeros((RWKV_LORA, W_AB), F32)
    params = [_row(mu), _row(w0), jnp.concatenate([w2, zeros]).astype(MXU_DTYPE), _row(a0),
              jnp.concatenate([zeros, a2]).astype(MXU_DTYPE), g2.astype(MXU_DTYPE),
              _row(k_k), _row(k_a), _row(r_k), _row(ln_w), _row(ln_b)]
    o_b = _rwkv(h, pre_gain, w_in[:, 4 * W_AB + 2 * nh:], params, seq, tb)

    w_out = w_out.astype(MXU_DTYPE)
    return [o_a, o_b], [w_out[:W_AB], w_out[W_AB:]]


def _odd_layer_mix(h, seq, pre_gain, w_in, gn_w, w_out):
    _, _, tb = _tiles(h.shape[0], seq)
    d = h.shape[1]
    wv = N_HEADS_RET * RET_DV
    perm = np.concatenate([np.concatenate([np.arange(0, RET_DK, 2), np.arange(1, RET_DK, 2)]) + hh * RET_DK
                           for hh in range(N_HEADS_RET)])
    w_in = w_in.astype(MXU_DTYPE)
    weights = [w_in[:, :d][:, perm], w_in[:, d:2 * d][:, perm], w_in[:, 2 * d:]]
    cos, sin = _rope_tables(seq)
    y = _retention(h, pre_gain, weights, cos, sin, _row(gn_w), seq, tb)
    return [y], [w_out.astype(MXU_DTYPE)]


def kernel(x, norm_mix_pre, norm_mix_post, norm_mlp_pre, norm_mlp_post, mlp_w_up, mlp_w_down, ab_w_in, gdn_conv_w, gdn_a_log, gdn_dt_bias, gdn_norm_w, rwkv_mu, rwkv_w0, rwkv_w2, rwkv_a0, rwkv_a2, rwkv_g2, rwkv_k_k, rwkv_k_a, rwkv_r_k, rwkv_ln_w, rwkv_ln_b, ab_w_out, ret_w_in, ret_gn_w, ret_w_out):
    batch, seq, d = x.shape
    h = x.astype(F32).reshape(batch * seq, d)
    tm = _tiles(batch * seq, seq)[0]
    for layer in range(norm_mix_pre.shape[0]):
        j = layer // 2
        if layer % 2 == 0:
            acts, w_outs = _even_layer_mix(h, seq, norm_mix_pre[layer], ab_w_in[j], gdn_conv_w[j],
                                           gdn_a_log[j], gdn_dt_bias[j], gdn_norm_w[j], rwkv_mu[j], rwkv_w0[j],
                                           rwkv_w2[j], rwkv_a0[j], rwkv_a2[j], rwkv_g2[j], rwkv_k_k[j],
                                           rwkv_k_a[j], rwkv_r_k[j], rwkv_ln_w[j], rwkv_ln_b[j], ab_w_out[j])
        else:
            acts, w_outs = _odd_layer_mix(h, seq, norm_mix_pre[layer], ret_w_in[j], ret_gn_w[j], ret_w_out[j])
        h = _mix_mlp(acts, w_outs, h, norm_mix_post[layer], norm_mlp_pre[layer],
                     mlp_w_up[layer].astype(MXU_DTYPE), mlp_w_down[layer].astype(MXU_DTYPE),
                     norm_mlp_post[layer], tm, MLP_FF_BLOCK)
    return h.reshape(batch, seq, d).astype(x.dtype)
```

```python
import functools

import numpy as np
import jax
import jax.numpy as jnp
from jax import lax
from jax.experimental import pallas as pl
from jax.experimental.pallas import tpu as pltpu

F32 = jnp.float32
BF16 = jnp.bfloat16
MXU_DTYPE = BF16

RMS_EPS = 1e-6
L2_EPS = 1e-6
HEAD_DIM = 64
N_HEADS_AB = 8
W_AB = N_HEADS_AB * HEAD_DIM
CHUNK = 64
SUB = 16
CONV_WIDTH = 4
RWKV_GN_EPS = 64e-5
RWKV_LORA = 64
RWKV_GATE_LORA = 128
N_HEADS_RET = 8
RET_DK = 128
RET_DV = 256
RET_CHUNK = 128
RET_GN_EPS = 1e-6
ROPE_BASE = 10000.0
LANES = 128
SUBLANES = 8
VMEM_LIMIT = 56 * 1024 * 1024
DENSE_ROWS = 512
MIXER_ROWS = 512
MLP_FF_BLOCK = 1024
RET_STEPS_PER_CHUNK = 6


def _mm(a, b):
    return jnp.dot(a.astype(MXU_DTYPE), b.astype(MXU_DTYPE), preferred_element_type=F32)


def _mm_nt(a, b):
    return lax.dot_general(a.astype(MXU_DTYPE), b.astype(MXU_DTYPE), (((1,), (1,)), ((), ())),
                           preferred_element_type=F32)


def _mm_tn(a, b):
    return lax.dot_general(a.astype(MXU_DTYPE), b.astype(MXU_DTYPE), (((0,), (0,)), ((), ())),
                           preferred_element_type=F32)


def _split(x, n):
    parts, r = [], x
    for i in range(n):
        p = r.astype(BF16)
        parts.append(p)
        if i + 1 < n:
            r = r - p.astype(F32)
    return parts


def _sel_r(x, e, n):
    acc = None
    for p in _split(x, n):
        d = jnp.dot(p, e, preferred_element_type=F32)
        acc = d if acc is None else acc + d
    return acc


def _sel_l(e, x, n):
    acc = None
    for p in _split(x, n):
        d = jnp.dot(e, p, preferred_element_type=F32)
        acc = d if acc is None else acc + d
    return acc


def _iota2(shape, dim):
    return lax.broadcasted_iota(jnp.int32, shape, dim)


def _head_ones(width, head):
    r, c = _iota2((width, width), 0), _iota2((width, width), 1)
    return jnp.where((r // head) == (c // head), 1.0, 0.0).astype(BF16)


def _head_sum(x, ones_g, n):
    w = ones_g.shape[0]
    return jnp.concatenate([_sel_r(x[:, i:i + w], ones_g, n) for i in range(0, x.shape[1], w)], axis=1)


def _chunk_tril(tb, chunk):
    r, c = _iota2((tb, tb), 0), _iota2((tb, tb), 1)
    return jnp.where(((r // chunk) == (c // chunk)) & (c <= r), 1.0, 0.0).astype(BF16)


def _chunk_cumsum(x, chunk, n):
    rows = min(x.shape[0], GROUP_W)
    tril = _chunk_tril(rows, chunk)
    return jnp.concatenate([_sel_l(tril, x[r:r + rows], n) for r in range(0, x.shape[0], rows)], axis=0)


def _chunk_last(x, tb, chunk):
    return jnp.concatenate(
        [jnp.broadcast_to(x[(c + 1) * chunk - 1:(c + 1) * chunk, :], (chunk, x.shape[1]))
         for c in range(tb // chunk)], axis=0)


def _sigmoid(x):
    return 1.0 / (1.0 + jnp.exp(-x))


def _softplus(x):
    return jnp.maximum(x, 0.0) + jnp.log1p(jnp.exp(-jnp.abs(x)))


GROUP = 4
GROUP_W = GROUP * HEAD_DIM
N_GROUPS = N_HEADS_AB // GROUP


def _packed_masks():
    c = CHUNK
    r, l = _iota2((c, GROUP_W), 0), _iota2((c, GROUP_W), 1) % c
    rr, cc = _iota2((GROUP_W, GROUP_W), 0), _iota2((GROUP_W, GROUP_W), 1)
    return dict(causal=r >= l, strict=r > l, eye=jnp.where(r == l, 1.0, 0.0).astype(F32),
                same_sub=(r // SUB) == (l // SUB), blk=(rr // c) == (cc // c))


def _bd(x, blk):
    xb = x.astype(MXU_DTYPE)
    return jnp.where(blk, jnp.concatenate([xb] * GROUP, axis=0), jnp.zeros((), MXU_DTYPE))


def _pmm(a, ybd):
    return jnp.dot(a.astype(MXU_DTYPE), ybd, preferred_element_type=F32)


def _bd_t(x, blk):
    wide = jnp.where(blk, jnp.concatenate([x.astype(F32)] * GROUP, axis=0), 0.0)
    return wide.T.astype(MXU_DTYPE)


def _pmm_nt(a, y, blk):
    return jnp.dot(a.astype(MXU_DTYPE), _bd_t(y, blk), preferred_element_type=F32)


def _each(f, *seqs):
    return [f(*args) for args in zip(*seqs)]


class _Interleave:
    def __init__(self, items):
        self._items = list(items)

    def __call__(self, n=1):
        for _ in range(min(n, len(self._items))):
            self._items.pop(0)()

    def flush(self):
        self(len(self._items))


def _project_ahead(h_ref, gain_ref, w_refs, u_s, p_s):
    u_s[...] = _rms(h_ref[...], gain_ref[...]).astype(u_s.dtype)

    def item(w_ref, lo, hi, base):
        def run():
            p_s[:, base + lo:base + hi] = jnp.dot(u_s[...], w_ref[:, lo:hi], preferred_element_type=F32)
        return run

    items, base = [], 0
    for w_ref in w_refs:
        n = w_ref.shape[1]
        items += [item(w_ref, lo, min(lo + GROUP_W, n), base) for lo in range(0, n, GROUP_W)]
        base += n
    return _Interleave(items)


def _lookahead_specs(n_blocks, tb, d):
    h_spec = pl.BlockSpec((tb, d), lambda i: (jnp.minimum(i, n_blocks - 1), 0))
    out_spec = lambda n: pl.BlockSpec((tb, n), lambda i: (jnp.maximum(i - 1, 0), 0))
    const = lambda shape: pl.BlockSpec(shape, lambda i: (0,) * len(shape), pipeline_mode=pl.Buffered(1))
    return h_spec, out_spec, const


def _starts_sequence(nt):
    i = pl.program_id(0)
    return (i == 0) | (lax.rem(i + nt - 1, nt) == 0)


def _inv_unit_lower(a, mk, fill):
    c = CHUNK

    def mm(xs, ys):
        out = _each(lambda x, y: _pmm(x, _bd(y, mk["blk"])), xs, ys)
        fill()
        return out

    def mm2(xs, zs, ys):
        out = mm(_each(lambda x, z: jnp.concatenate([x, z], axis=0), xs, zs), ys)
        return [o[:c] for o in out], [o[c:] for o in out]

    add = lambda xs, ys: _each(jnp.add, xs, ys)
    d = _each(lambda x: jnp.where(mk["same_sub"], x, 0.0), a)
    l = _each(jnp.subtract, a, d)
    d2 = mm(d, d)
    p = _each(lambda x: mk["eye"] - x, d)
    pd, d4 = mm2(p, d2, d2)
    p = add(p, pd)
    pd, d8 = mm2(p, d4, d4)
    p = add(p, pd)
    td = add(p, mm(p, d8))
    n = mm(l, td)
    n2, tn = mm2(n, td, n)
    y = _each(jnp.subtract, td, tn)
    return add(y, mm(y, n2))


def _scan_scratch(tb):
    nc = tb // CHUNK
    per = lambda rows, dt: pltpu.VMEM((nc, N_GROUPS, rows, GROUP_W), dt)
    return [pltpu.VMEM((N_GROUPS, GROUP_W, GROUP_W), F32), per(CHUNK, MXU_DTYPE), per(CHUNK, F32),
            per(GROUP_W, MXU_DTYPE), per(GROUP_W, F32), pltpu.VMEM((tb, W_AB), F32)]


def _rms(x, g):
    return x * lax.rsqrt(jnp.mean(x * x, axis=-1, keepdims=True) + RMS_EPS) * g


def _mix_mlp_kernel(*refs, n_in, ff_blk):
    a_refs, w_refs = refs[:n_in], refs[n_in:2 * n_in]
    h_ref, gmix_ref, g1_ref, wu_ref, wd_ref, g2_ref, o_ref = refs[2 * n_in:]
    mix = None
    for a_ref, w_ref in zip(a_refs, w_refs):
        d = jnp.dot(a_ref[...], w_ref[...], preferred_element_type=F32)
        mix = d if mix is None else mix + d
    x = h_ref[...] + _rms(mix, gmix_ref[...])
    u = _rms(x, g1_ref[...]).astype(MXU_DTYPE)
    acc = None
    for j in range(wu_ref.shape[1] // ff_blk):
        a = jnp.dot(u, wu_ref[:, j * ff_blk:(j + 1) * ff_blk], preferred_element_type=F32)
        a = jnp.square(jnp.maximum(a, 0.0)).astype(MXU_DTYPE)
        d = jnp.dot(a, wd_ref[j * ff_blk:(j + 1) * ff_blk, :], preferred_element_type=F32)
        acc = d if acc is None else acc + d
    o_ref[...] = x + _rms(acc, g2_ref[...])


def _mix_mlp(acts, w_outs, h, g_mix, g1, w_up, w_down, g2, tm, ff_blk):
    m, d = h.shape
    n_in = len(acts)
    rows = lambda n: pl.BlockSpec((tm, n), lambda i: (i, 0))
    resident = lambda shape: pl.BlockSpec(shape, lambda i: (0, 0), pipeline_mode=pl.Buffered(1))
    gains = [g.reshape(1, d) for g in (g_mix, g1, g2)]
    return pl.pallas_call(
        functools.partial(_mix_mlp_kernel, n_in=n_in, ff_blk=ff_blk),
        grid=(m // tm,),
        in_specs=[rows(a.shape[1]) for a in acts] + [resident(w.shape) for w in w_outs]
                 + [rows(d), resident((1, d)), resident((1, d)),
                    resident(w_up.shape), resident(w_down.shape), resident((1, d))],
        out_specs=rows(d),
        out_shape=jax.ShapeDtypeStruct((m, d), F32),
        compiler_params=pltpu.CompilerParams(dimension_semantics=("parallel",),
                                             vmem_limit_bytes=VMEM_LIMIT),
        name="mix_mlp",
    )(*acts, *w_outs, h, gains[0], gains[1], w_up, w_down, gains[2])


def _gdn_kernel(h_ref, gain_ref, w_ref, cw_ref, gp_ref, nw_ref, o_ref,
                p_s, u_s, z_s, xp_ref, s_ref, qeff_s, o0_s, mt_s, n0_s, o_s, *, tb, nt):
    nh, hd, c = N_HEADS_AB, HEAD_DIM, CHUNK
    nc = tb // c

    @pl.when(pl.program_id(0) == 0)
    def _():
        p_s[...] = jnp.zeros_like(p_s)

    @pl.when(_starts_sequence(nt))
    def _():
        s_ref[...] = jnp.zeros_like(s_ref)
        xp_ref[0:SUBLANES, :] = jnp.zeros((SUBLANES, xp_ref.shape[1]), F32)

    xp_ref[SUBLANES:SUBLANES + tb, :] = p_s[:, 0:3 * W_AB]
    z_s[...] = p_s[:, 3 * W_AB:4 * W_AB]
    gt = p_s[:, 4 * W_AB:]
    fill = _project_ahead(h_ref, gain_ref, [w_ref], u_s, p_s)

    cw = cw_ref[...]
    ones_h = _head_ones(GROUP_W, hd)
    slabs = []
    for lo in range(0, 3 * W_AB, GROUP_W):
        cols = slice(lo, lo + GROUP_W)
        y = xp_ref[SUBLANES:SUBLANES + tb, cols] * cw[CONV_WIDTH - 1:CONV_WIDTH, cols]
        for j in range(CONV_WIDTH - 1):
            y = y + xp_ref[pl.ds(SUBLANES - (CONV_WIDTH - 1) + j, tb), cols] * cw[j:j + 1, cols]
        y = y * _sigmoid(y)
        if lo < 2 * W_AB:
            y = y * lax.rsqrt(_sel_r(y * y, ones_h, 1) + L2_EPS)
        slabs.append(y)
        fill()
    xp_ref[0:SUBLANES, :] = xp_ref[tb:tb + SUBLANES, :]
    per = W_AB // GROUP_W
    qn = jnp.concatenate(slabs[0:per], axis=1) * (hd ** -0.5)
    kn = jnp.concatenate(slabs[per:2 * per], axis=1)
    v = jnp.concatenate(slabs[2 * per:], axis=1)

    gp = gp_ref[...]
    beta = _sigmoid(gt)
    g = -jnp.exp(gp[0:1, :]) * _softplus(gt + gp[1:2, :])
    gcum = _chunk_cumsum(g, c, 3)
    r_, c_ = _iota2((LANES, W_AB), 0), _iota2((LANES, W_AB), 1)
    pick_beta = jnp.where(r_ == c_ // hd, 1.0, 0.0).astype(BF16)
    pick_g = jnp.where(r_ == nh + c_ // hd, 1.0, 0.0).astype(BF16)
    beta_e = _sel_r(beta, pick_beta, 2)
    gc_e = _sel_r(gcum, pick_g, 3)
    glast_e = _chunk_last(gc_e, tb, c)
    gct = gcum.T
    fill()

    eg = jnp.exp(gc_e)
    kb = kn * beta_e
    kbe, vb, qd = kb * eg, v * beta_e, qn * eg
    fill()
    kd = kn * jnp.exp(glast_e - gc_e)
    blk_decay = jnp.exp(glast_e)
    fill()

    mk = _packed_masks()
    blk = mk["blk"]

    probs = [(ci, gi) for ci in range(nc) for gi in range(N_GROUPS)]
    tile = lambda x: [x[ci * c:(ci + 1) * c, gi * GROUP_W:(gi + 1) * GROUP_W] for ci, gi in probs]
    g_rows = [jnp.concatenate([gct[nh + h:nh + h + 1, ci * c:(ci + 1) * c] for h in range(nh)], axis=1)
              for ci in range(nc)]
    diff = _each(lambda gcol, p: gcol - g_rows[p[0]][:, p[1] * GROUP_W:(p[1] + 1) * GROUP_W], tile(gc_e), probs)
    dec = _each(lambda x: jnp.where(mk["causal"], jnp.exp(jnp.where(mk["causal"], x, 0.0)), 0.0), diff)
    both = _each(lambda kb_, q_, k_: _pmm_nt(jnp.concatenate([kb_, q_], axis=0), k_, blk),
                 tile(kb), tile(qn), tile(kn))
    a = _each(lambda x, e: jnp.where(mk["strict"], x[:c] * e, 0.0), both, dec)
    attn = _each(lambda x, e: x[c:] * e, both, dec)
    t = _inv_unit_lower(a, mk, fill)
    fill.flush()
    w = _each(lambda t_, x: _pmm(t_, _bd(x, blk)), t, tile(kbe))
    u = _each(lambda t_, x: _pmm(t_, _bd(x, blk)), t, tile(vb))
    aw = _each(lambda at, x: _pmm(at, _bd(x, blk)), attn, w)
    o0 = _each(lambda at, x: _pmm(at, _bd(x, blk)), attn, u)
    mt = _each(lambda kd_, x: jnp.where(blk, _mm_tn(kd_, x), 0.0), tile(kd), w)
    n0 = _each(lambda kd_, x: jnp.where(blk, _mm_tn(kd_, x), 0.0), tile(kd), u)
    for (ci, gi), qd_, aw_, o0_, mt_, n0_ in zip(probs, tile(qd), aw, o0, mt, n0):
        qeff_s[ci, gi] = (qd_ - aw_).astype(qeff_s.dtype)
        o0_s[ci, gi] = o0_
        mt_s[ci, gi] = mt_.astype(mt_s.dtype)
        n0_s[ci, gi] = n0_

    for ci in range(nc):
        rs = slice(ci * c, (ci + 1) * c)
        for gi in range(N_GROUPS):
            ls = slice(gi * GROUP_W, (gi + 1) * GROUP_W)
            s = s_ref[gi]
            sb = s.astype(MXU_DTYPE)
            o_s[rs, ls] = jnp.dot(qeff_s[ci, gi], sb, preferred_element_type=F32) + o0_s[ci, gi]
            s_ref[gi] = (s * blk_decay[ci * c:ci * c + 1, ls]
                         - jnp.dot(mt_s[ci, gi], sb, preferred_element_type=F32) + n0_s[ci, gi])

    o = o_s[...]
    o = o * lax.rsqrt(_head_sum(o * o, ones_h, 1) * (1.0 / hd) + RMS_EPS) * nw_ref[...]
    z = z_s[...]
    o_ref[...] = (o * (z * _sigmoid(z))).astype(o_ref.dtype)


def _gdn(h, gain, w, conv_w, gate_params, norm_w, seq, tb):
    m, d = h.shape
    nt, n_blocks = seq // tb, m // tb
    h_spec, out_spec, const = _lookahead_specs(n_blocks, tb, d)
    return pl.pallas_call(
        functools.partial(_gdn_kernel, tb=tb, nt=nt),
        grid=(n_blocks + 1,),
        in_specs=[h_spec, const((1, d)), const(w.shape),
                  const(conv_w.shape), const(gate_params.shape), const(norm_w.shape)],
        out_specs=out_spec(W_AB),
        out_shape=jax.ShapeDtypeStruct((m, W_AB), BF16),
        scratch_shapes=[pltpu.VMEM((tb, w.shape[1]), F32), pltpu.VMEM((tb, d), MXU_DTYPE),
                        pltpu.VMEM((tb, W_AB), F32), pltpu.VMEM((SUBLANES + tb, 3 * W_AB), F32)]
                       + _scan_scratch(tb),
        compiler_params=pltpu.CompilerParams(dimension_semantics=("arbitrary",),
                                             vmem_limit_bytes=VMEM_LIMIT),
        name="gdn",
    )(h, gain.reshape(1, d), w, conv_w, gate_params, norm_w)


def _rwkv_kernel(h_ref, gain_ref, w_ref, mu_ref, w0_ref, w2_ref, a0_ref, a2_ref, g2_ref, kk_ref, ka_ref,
                 rk_ref, lnw_ref, lnb_ref, o_ref,
                 p_s, u_s, xp_ref, s_ref, qeff_s, y0_s, mt_s, n0_s, y_s, *, tb, nt):
    nh, hd, c = N_HEADS_AB, HEAD_DIM, CHUNK
    nc = tb // c

    @pl.when(pl.program_id(0) == 0)
    def _():
        p_s[...] = jnp.zeros_like(p_s)

    @pl.when(_starts_sequence(nt))
    def _():
        s_ref[...] = jnp.zeros_like(s_ref)
        xp_ref[0:SUBLANES, :] = jnp.zeros((SUBLANES, xp_ref.shape[1]), F32)

    rp = p_s[...]
    fill = _project_ahead(h_ref, gain_ref, [w_ref], u_s, p_s)
    xp_ref[SUBLANES:SUBLANES + tb, :] = rp
    prev = xp_ref[pl.ds(SUBLANES - 1, tb), :]
    xp_ref[0:SUBLANES, :] = rp[tb - SUBLANES:tb, :]
    xs = rp + (prev - rp) * mu_ref[...]
    r, kr, vr = xs[:, 0:W_AB], xs[:, W_AB:2 * W_AB], xs[:, 2 * W_AB:3 * W_AB]
    xwa = xs[:, 3 * W_AB:3 * W_AB + 2 * RWKV_LORA]
    xg = xs[:, 3 * W_AB + 2 * RWKV_LORA:]

    fill()
    tanh_xwa, sig_xg = jnp.tanh(xwa).astype(MXU_DTYPE), _sigmoid(xg).astype(MXU_DTYPE)
    xwa_b = xwa.astype(MXU_DTYPE)
    ones_h = _head_ones(GROUP_W, hd)

    names = ("gate", "k", "rh", "kkh", "bh", "kh", "bt", "kt", "decay")
    parts = {n: [] for n in names}
    for lo in range(0, W_AB, GROUP_W):
        cols = slice(lo, lo + GROUP_W)
        w_lora = jnp.dot(tanh_xwa, w2_ref[:, cols], preferred_element_type=F32)
        lw = -jnp.exp(-_softplus(-(w0_ref[:, cols] + w_lora)) - 0.5)
        a = _sigmoid(a0_ref[:, cols] + jnp.dot(xwa_b, a2_ref[:, cols], preferred_element_type=F32))
        parts["gate"].append(jnp.dot(sig_xg, g2_ref[:, cols], preferred_element_type=F32))
        fill()
        kk = kr[:, cols] * kk_ref[:, cols]
        kk = kk * lax.rsqrt(_sel_r(kk * kk, ones_h, 1) + L2_EPS)
        k = kr[:, cols] * (1.0 + (a - 1.0) * ka_ref[:, cols])
        b = kk * a
        cum = _chunk_cumsum(lw, c, 2)
        last = _chunk_last(cum, tb, c)
        e_neg = jnp.exp(-cum)
        e_rem = jnp.exp(last - cum)
        for n, val in (("k", k), ("rh", r[:, cols] * jnp.exp(cum)), ("kkh", kk * jnp.exp(cum - lw)),
                       ("bh", b * e_neg), ("kh", k * e_neg), ("bt", b * e_rem), ("kt", k * e_rem),
                       ("decay", jnp.exp(last))):
            parts[n].append(val)
        fill()
    whole = {n: jnp.concatenate(v, axis=1) for n, v in parts.items()}
    gate, k, chunk_decay = whole["gate"], whole["k"], whole["decay"]
    rh_all, kkh_all, bh_all, kh_all = whole["rh"], whole["kkh"], whole["bh"], whole["kh"]
    bt_all, kt_all = whole["bt"], whole["kt"]

    mk = _packed_masks()
    blk, causal, strict = mk["blk"], mk["causal"], mk["strict"]

    probs = [(ci, gi) for ci in range(nc) for gi in range(N_GROUPS)]
    tile = lambda x: [x[ci * c:(ci + 1) * c, gi * GROUP_W:(gi + 1) * GROUP_W] for ci, gi in probs]
    rh, kkh, vv, bt, kt = tile(rh_all), tile(kkh_all), tile(vr), tile(bt_all), tile(kt_all)
    lhs = _each(lambda x, y: jnp.concatenate([x, y], axis=0), kkh, rh)
    nt = lambda x, y: lax.dot_general(x.astype(MXU_DTYPE), _bd(y, blk), (((1,), (1,)), ((), ())),
                                      preferred_element_type=F32)
    ab = _each(nt, lhs, tile(bh_all))
    ak = _each(nt, lhs, tile(kh_all))
    a_bb = _each(lambda x: jnp.where(strict, x[:c], 0.0), ab)
    a_rb = _each(lambda x: jnp.where(causal, x[c:], 0.0), ab)
    a_k = _each(lambda x: jnp.concatenate([jnp.where(strict, x[:c], 0.0), jnp.where(causal, x[c:], 0.0)],
                                          axis=0), ak)
    t = _inv_unit_lower(a_bb, mk, fill)
    fill.flush()
    z0 = _each(lambda x, y: _pmm(x, _bd(y, blk)), a_k, vv)
    wk = _each(lambda x, y: _pmm(x, _bd(y, blk)), t, kkh)
    u0 = _each(lambda x, y: -_pmm(x, _bd(y[:c], blk)), t, z0)
    ar = _each(lambda x, y: _pmm(x, _bd(y, blk)), a_rb, wk)
    y0 = _each(lambda x, y, z: _pmm(x, _bd(y, blk)) + z[c:], a_rb, u0, z0)
    mt = _each(lambda x, y: jnp.where(blk, _mm_tn(x, y), 0.0), wk, bt)
    n0 = _each(lambda u_, v_, b_, k_: jnp.where(blk, _mm_tn(jnp.concatenate([u_, v_], axis=0),
                                                            jnp.concatenate([b_, k_], axis=0)), 0.0),
               u0, vv, bt, kt)
    for (ci, gi), rh_, ar_, y0_, mt_, n0_ in zip(probs, rh, ar, y0, mt, n0):
        qeff_s[ci, gi] = (rh_ - ar_).astype(qeff_s.dtype)
        y0_s[ci, gi] = y0_
        mt_s[ci, gi] = mt_.astype(mt_s.dtype)
        n0_s[ci, gi] = n0_

    for ci in range(nc):
        rs = slice(ci * c, (ci + 1) * c)
        for gi in range(N_GROUPS):
            ls = slice(gi * GROUP_W, (gi + 1) * GROUP_W)
            s = s_ref[gi]
            sb = s.astype(MXU_DTYPE)
            y_s[rs, ls] = lax.dot_general(qeff_s[ci, gi], sb, (((1,), (1,)), ((), ())),
                                          preferred_element_type=F32) + y0_s[ci, gi]
            s_ref[gi] = (s * chunk_decay[ci * c:ci * c + 1, ls]
                         - jnp.dot(sb, mt_s[ci, gi], preferred_element_type=F32) + n0_s[ci, gi])

    y = y_s[...]
    inv_hd = 1.0 / hd
    yc = y - _head_sum(y, ones_h, 2) * inv_hd
    yn = yc * lax.rsqrt(_head_sum(yc * yc, ones_h, 1) * inv_hd + RWKV_GN_EPS) * lnw_ref[...] + lnb_ref[...]
    bonus = _head_sum(r * k * rk_ref[...], ones_h, 1) * vr
    o_ref[...] = ((yn + bonus) * gate).astype(o_ref.dtype)


def _rwkv(h, gain, w, params, seq, tb):
    m, d = h.shape
    n_in = w.shape[1]
    nt, n_blocks = seq // tb, m // tb
    h_spec, out_spec, const = _lookahead_specs(n_blocks, tb, d)
    return pl.pallas_call(
        functools.partial(_rwkv_kernel, tb=tb, nt=nt),
        grid=(n_blocks + 1,),
        in_specs=[h_spec, const((1, d)), const(w.shape)] + [const(p.shape) for p in params],
        out_specs=out_spec(W_AB),
        out_shape=jax.ShapeDtypeStruct((m, W_AB), BF16),
        scratch_shapes=[pltpu.VMEM((tb, n_in), F32), pltpu.VMEM((tb, d), MXU_DTYPE),
                        pltpu.VMEM((SUBLANES + tb, n_in), F32)] + _scan_scratch(tb),
        compiler_params=pltpu.CompilerParams(dimension_semantics=("arbitrary",),
                                             vmem_limit_bytes=VMEM_LIMIT),
        name="rwkv7",
    )(h, gain.reshape(1, d), w, *params)


def _rope_table_kernel(angle_ref, cos_ref, sin_ref):
    shape = cos_ref.shape
    pos = _iota2(shape, 0).astype(F32)
    theta = pos * angle_ref[...]
    cos_ref[...] = jnp.cos(theta)
    s = jnp.sin(theta)
    sin_ref[...] = jnp.where(_iota2(shape, 1) < RET_DK // 2, -s, s)


def _rope_tables(seq):
    angle = 1.0 / (ROPE_BASE ** jnp.linspace(0.0, 1.0, RET_DK // 2, dtype=F32))
    angle = jnp.concatenate([angle, angle]).reshape(1, RET_DK)
    return pl.pallas_call(
        _rope_table_kernel,
        out_shape=[jax.ShapeDtypeStruct((seq, RET_DK), F32)] * 2,
        name="rope_tables",
    )(angle)


def _ret_log_gamma():
    return [float(np.log1p(-np.exp2(np.float32(-5.0 - h)))) for h in range(N_HEADS_RET)]


def _ret_kernel(h_ref, gain_ref, wq_ref, wk_ref, wvg_ref, cos_ref, sin_ref, gnw_ref, o_ref,
                p_s, u_s, q_s, k_s, v_s, g_s, s_ref, intra_s, qdec_s, kdec_s, *, tb, nt):
    dk, dv, nh, c = RET_DK, RET_DV, N_HEADS_RET, RET_CHUNK
    wq, wv = nh * dk, nh * dv
    log_gamma = _ret_log_gamma()

    @pl.when(pl.program_id(0) == 0)
    def _():
        p_s[...] = jnp.zeros_like(p_s)
        rel = (_iota2((c, c), 0) - _iota2((c, c), 1)).astype(F32)
        idx = _iota2((c, dk), 0).astype(F32)
        for h, lg in enumerate(log_gamma):
            intra_s[h] = jnp.where(rel >= 0.0, jnp.exp(lg * jnp.maximum(rel, 0.0)), 0.0)
            qdec_s[h] = jnp.exp(lg * (idx + 1.0))
            kdec_s[h] = jnp.exp(lg * (c - 1.0 - idx))

    @pl.when(_starts_sequence(nt))
    def _():
        s_ref[...] = jnp.zeros_like(s_ref)

    fill = _project_ahead(h_ref, gain_ref, [wq_ref, wk_ref, wvg_ref], u_s, p_s)
    cos, sin = cos_ref[...], sin_ref[...]
    for n_slab, lo in enumerate(range(0, p_s.shape[1], GROUP_W)):
        x = p_s[:, lo:lo + GROUP_W]
        if lo < 2 * wq:
            dst, off, scale = (q_s, lo, None) if lo < wq else (k_s, lo - wq, dk ** -0.5)
            for j in range(0, GROUP_W, dk):
                xh = x[:, j:j + dk]
                xr = xh * cos + pltpu.roll(xh, dk // 2, 1) * sin
                dst[:, off + j:off + j + dk] = xr if scale is None else xr * scale
        elif lo < 2 * wq + wv:
            v_s[:, lo - 2 * wq:lo - 2 * wq + GROUP_W] = x.astype(v_s.dtype)
        else:
            g_s[:, lo - 2 * wq - wv:lo - 2 * wq - wv + GROUP_W] = x * _sigmoid(x)
        if n_slab % 3 == 0:
            fill()

    n_points = RET_STEPS_PER_CHUNK * (tb // c)
    n_left = p_s.shape[1] // GROUP_W - (p_s.shape[1] // GROUP_W + 2) // 3
    share = iter([n_left * (k + 1) // n_points - n_left * k // n_points for k in range(n_points)])
    heads = list(range(nh))
    for ci in range(tb // c):
        rows = slice(ci * c, (ci + 1) * c)
        q = [q_s[rows, h * dk:(h + 1) * dk] for h in heads]
        k = [k_s[rows, h * dk:(h + 1) * dk] for h in heads]
        v = [v_s[rows, h * dv:(h + 1) * dv] for h in heads]
        s = [s_ref[h] for h in heads]
        scores = _each(lambda q_, k_, h: _mm_nt(q_, k_) * intra_s[h], q, k, heads)
        fill(next(share))
        cross = _each(lambda q_, s_, h: _mm(q_ * qdec_s[h], s_), q, s, heads)
        fill(next(share))
        inner = _each(_mm, scores, v)
        fill(next(share))
        upd = _each(lambda k_, v_, h: _mm_tn(k_ * kdec_s[h], v_), k, v, heads)
        fill(next(share))
        for h, lg in enumerate(log_gamma):
            s_ref[h] = s[h] * float(np.exp(lg * c)) + upd[h]
        y = _each(jnp.add, inner, cross)
        yc = _each(lambda y_: y_ - jnp.mean(y_, axis=-1, keepdims=True), y)
        fill(next(share))
        yn = _each(lambda y_: y_ * lax.rsqrt(jnp.mean(y_ * y_, axis=-1, keepdims=True) + RET_GN_EPS), yc)
        fill(next(share))
        for h in heads:
            cols = slice(h * dv, (h + 1) * dv)
            o_ref[rows, cols] = (g_s[rows, cols] * yn[h] * gnw_ref[:, cols]).astype(o_ref.dtype)
    fill.flush()


def _retention(h, gain, weights, cos, sin, gn_w, seq, tb):
    m, d = h.shape
    nt, n_blocks = seq // tb, m // tb
    nh, dk, dv, c = N_HEADS_RET, RET_DK, RET_DV, RET_CHUNK
    h_spec, out_spec, const = _lookahead_specs(n_blocks, tb, d)
    pos_blk = pl.BlockSpec((tb, dk), lambda i: (lax.rem(jnp.maximum(i - 1, 0), nt), 0))
    return pl.pallas_call(
        functools.partial(_ret_kernel, tb=tb, nt=nt),
        grid=(n_blocks + 1,),
        in_specs=[h_spec, const((1, d))] + [const(w.shape) for w in weights]
                 + [pos_blk, pos_blk, const(gn_w.shape)],
        out_specs=out_spec(nh * dv),
        out_shape=jax.ShapeDtypeStruct((m, nh * dv), BF16),
        scratch_shapes=[pltpu.VMEM((tb, sum(w.shape[1] for w in weights)), F32), pltpu.VMEM((tb, d), MXU_DTYPE),
                        pltpu.VMEM((tb, nh * dk), F32), pltpu.VMEM((tb, nh * dk), F32),
                        pltpu.VMEM((tb, nh * dv), MXU_DTYPE), pltpu.VMEM((tb, nh * dv), F32),
                        pltpu.VMEM((nh, dk, dv), F32), pltpu.VMEM((nh, c, c), F32),
                        pltpu.VMEM((nh, c, dk), F32), pltpu.VMEM((nh, c, dk), F32)],
        compiler_params=pltpu.CompilerParams(dimension_semantics=("arbitrary",),
                                             vmem_limit_bytes=VMEM_LIMIT),
        name="retention",
    )(h, gain.reshape(1, d), *weights, cos, sin, gn_w)


def _tiles(m, seq):
    assert m % DENSE_ROWS == 0 and seq % MIXER_ROWS == 0, (m, seq)
    return DENSE_ROWS, MIXER_ROWS, MIXER_ROWS


def _row(v):
    return v.reshape(1, -1).astype(F32)


def _even_layer_mix(h, seq, pre_gain, w_in, conv_w, a_log, dt_bias, gdn_norm_w,
                    mu, w0, w2, a0, a2, g2, k_k, k_a, r_k, ln_w, ln_b, w_out):
    _, tb, _ = _tiles(h.shape[0], seq)
    nh = N_HEADS_AB
    w_in = w_in.astype(MXU_DTYPE)
    gate_cols = jnp.pad(w_in[:, 4 * W_AB:4 * W_AB + 2 * nh], ((0, 0), (0, LANES - 2 * nh)))
    w_gdn = jnp.concatenate([w_in[:, :4 * W_AB], gate_cols], axis=1)
    pad_g = lambda v: jnp.pad(v.astype(F32), (nh, LANES - 2 * nh))
    gate_params = jnp.stack([pad_g(a_log), pad_g(dt_bias)])
    o_a = _gdn(h, pre_gain, w_gdn, conv_w.astype(F32), gate_params,
               _row(jnp.tile(gdn_norm_w, nh)), seq, tb)

    zeros = jnp.zeros((RWKV_LORA, W_AB), F32)
    params = [_row(mu), _row(w0), jnp.concatenate([w2, zeros]).astype(MXU_DTYPE), _row(a0),
              jnp.concatenate([zeros, a2]).astype(MXU_DTYPE), g2.astype(MXU_DTYPE),
              _row(k_k), _row(k_a), _row(r_k), _row(ln_w), _row(ln_b)]
    o_b = _rwkv(h, pre_gain, w_in[:, 4 * W_AB + 2 * nh:], params, seq, tb)

    w_out = w_out.astype(MXU_DTYPE)
    return [o_a, o_b], [w_out[:W_AB], w_out[W_AB:]]


def _odd_layer_mix(h, seq, pre_gain, w_in, gn_w, w_out):
    _, _, tb = _tiles(h.shape[0], seq)
    d = h.shape[1]
    wv = N_HEADS_RET * RET_DV
    perm = np.concatenate([np.concatenate([np.arange(0, RET_DK, 2), np.arange(1, RET_DK, 2)]) + hh * RET_DK
                           for hh in range(N_HEADS_RET)])
    w_in = w_in.astype(MXU_DTYPE)
    weights = [w_in[:, :d][:, perm], w_in[:, d:2 * d][:, perm], w_in[:, 2 * d:]]
    cos, sin = _rope_tables(seq)
    y = _retention(h, pre_gain, weights, cos, sin, _row(gn_w), seq, tb)
    return [y], [w_out.astype(MXU_DTYPE)]


def kernel(x, norm_mix_pre, norm_mix_post, norm_mlp_pre, norm_mlp_post, mlp_w_up, mlp_w_down, ab_w_in, gdn_conv_w, gdn_a_log, gdn_dt_bias, gdn_norm_w, rwkv_mu, rwkv_w0, rwkv_w2, rwkv_a0, rwkv_a2, rwkv_g2, rwkv_k_k, rwkv_k_a, rwkv_r_k, rwkv_ln_w, rwkv_ln_b, ab_w_out, ret_w_in, ret_gn_w, ret_w_out):
    batch, seq, d = x.shape
    h = x.astype(F32).reshape(batch * seq, d)
    tm = _tiles(batch * seq, seq)[0]
    for layer in range(norm_mix_pre.shape[0]):
        j = layer // 2
        if layer % 2 == 0:
            acts, w_outs = _even_layer_mix(h, seq, norm_mix_pre[layer], ab_w_in[j], gdn_conv_w[j],
                                           gdn_a_log[j], gdn_dt_bias[j], gdn_norm_w[j], rwkv_mu[j], rwkv_w0[j],
                                           rwkv_w2[j], rwkv_a0[j], rwkv_a2[j], rwkv_g2[j], rwkv_k_k[j],
                                           rwkv_k_a[j], rwkv_r_k[j], rwkv_ln_w[j], rwkv_ln_b[j], ab_w_out[j])
        else:
            acts, w_outs = _odd_layer_mix(h, seq, norm_mix_pre[layer], ret_w_in[j], ret_gn_w[j], ret_w_out[j])
        h = _mix_mlp(acts, w_outs, h, norm_mix_post[layer], norm_mlp_pre[layer],
                     mlp_w_up[layer].astype(MXU_DTYPE), mlp_w_down[layer].astype(MXU_DTYPE),
                     norm_mlp_post[layer], tm, MLP_FF_BLOCK)
    return h.reshape(batch, seq, d).astype(x.dtype)
```
